```python
import jax, jax.numpy as jnp
from jax import lax
import numpy as np

D_MODEL = 1024
BATCH = 4
SEQ = 4096
DEPTH = 1

D_CONV = D_MODEL // 2
D_ATTN = D_MODEL - D_CONV
D_MIX = D_CONV + D_ATTN
HEAD_DIM = 64
N_HEADS = D_ATTN // HEAD_DIM
CONV_GROUPS = D_CONV // HEAD_DIM
CONV_WIDTH = 3
ROPE_DIM = HEAD_DIM // 4
ROPE_THETA = 500000.0
N_IDX_HEADS = 8
IDX_DIM = 64
TOPK_MAX = 256
Q_BLOCK = 128
EPS = 1e-6

kernel_name = "hybrid_conv_dsa_parallel_heads"

SPLIT_SIZES = [D_CONV, D_CONV, D_CONV, D_CONV,
               D_ATTN, D_ATTN, D_ATTN, D_ATTN,
               N_IDX_HEADS * IDX_DIM, IDX_DIM, N_IDX_HEADS]
D_IN = sum(SPLIT_SIZES)


def rmsnorm(x, gain):
    xf = x.astype(jnp.float32)
    inv = lax.rsqrt(jnp.mean(xf * xf, axis=-1, keepdims=True) + EPS)
    return (xf * inv).astype(x.dtype) * gain


def group_rmsnorm(y, gain, n_groups):
    shp = y.shape
    yg = y.reshape(shp[:-1] + (n_groups, shp[-1] // n_groups)).astype(jnp.float32)
    inv = lax.rsqrt(jnp.mean(yg * yg, axis=-1, keepdims=True) + EPS)
    return (yg * inv).reshape(shp).astype(y.dtype) * gain


def rope_partial(x, positions):
    half = ROPE_DIM // 2
    inv_freq = ROPE_THETA ** (-jnp.arange(half, dtype=jnp.float32) / half)
    ang = positions.astype(jnp.float32)[..., None] * inv_freq
    cos = jnp.cos(ang)[:, :, None, :]
    sin = jnp.sin(ang)[:, :, None, :]
    xr = x[..., :ROPE_DIM].astype(jnp.float32)
    x1, x2 = xr[..., :half], xr[..., half:]
    rot = jnp.concatenate([x1 * cos - x2 * sin, x1 * sin + x2 * cos], axis=-1).astype(x.dtype)
    return jnp.concatenate([rot, x[..., ROPE_DIM:]], axis=-1)


def short_conv_mixer(b_gate, c_gate, u, conv_w, conv_b):
    s = c_gate * u
    L = s.shape[1]
    sp = jnp.pad(s, ((0, 0), (CONV_WIDTH - 1, 0), (0, 0)))
    conv = conv_b + sum(conv_w[j] * sp[:, j:j + L, :] for j in range(CONV_WIDTH))
    return b_gate * conv


def dsa_sparse_attention(q, k, v, q_idx, k_idx, w_idx):
    Bsz, L = q.shape[0], q.shape[1]
    topk = min(TOPK_MAX, L // 4)
    n_blocks = L // Q_BLOCK
    idx_scale = IDX_DIM ** -0.5
    att_scale = HEAD_DIM ** -0.5
    key_pos = jnp.arange(L)
    gather = jax.vmap(lambda kb, ib: kb[ib])

    def block(i):
        t0 = i * Q_BLOCK
        q_b = lax.dynamic_slice_in_dim(q, t0, Q_BLOCK, axis=1)
        qi_b = lax.dynamic_slice_in_dim(q_idx, t0, Q_BLOCK, axis=1)
        wi_b = lax.dynamic_slice_in_dim(w_idx, t0, Q_BLOCK, axis=1)
        tq = t0 + jnp.arange(Q_BLOCK)
        dots = jnp.einsum('bqhd,bsd->bqhs', qi_b, k_idx) * idx_scale
        score = jnp.einsum('bqh,bqhs->bqs', wi_b.astype(jnp.float32),
                           jax.nn.relu(dots).astype(jnp.float32))
        causal = key_pos[None, :] <= tq[:, None]
        score = jnp.where(causal[None], score, -jnp.inf)
        _, sel = lax.top_k(score, topk)
        valid = sel <= tq[None, :, None]
        k_sel = gather(k, sel)
        v_sel = gather(v, sel)
        logits = jnp.einsum('bqhd,bqkhd->bqhk', q_b, k_sel).astype(jnp.float32) * att_scale
        logits = jnp.where(valid[:, :, None, :], logits, -jnp.inf)
        p = jax.nn.softmax(logits, axis=-1).astype(v.dtype)
        return jnp.einsum('bqhk,bqkhd->bqhd', p, v_sel)

    outs = lax.map(block, jnp.arange(n_blocks))
    return jnp.transpose(outs, (1, 0, 2, 3, 4)).reshape(Bsz, L, N_HEADS * HEAD_DIM)


def hybrid_layer(x, c, positions, w_ada, b_ada, norm_in, w_in, conv_w, conv_b,
                 gn_conv, gn_attn, w_out):
    Bsz, L, _ = x.shape
    mod = jax.nn.silu(c) @ w_ada + b_ada
    shift, scale, gate = jnp.split(mod, 3, axis=-1)
    h = rmsnorm(x, norm_in) * (1.0 + scale[:, None, :]) + shift[:, None, :]

    proj = h @ w_in
    cuts = [int(v) for v in np.cumsum(SPLIT_SIZES)[:-1]]
    (cb, cc, cu, cz, aq, ak, av, az, iq, ik, iw) = jnp.split(proj, cuts, axis=-1)

    y_conv = short_conv_mixer(cb, cc, cu, conv_w, conv_b)
    y_conv = group_rmsnorm(y_conv, gn_conv, CONV_GROUPS) * jax.nn.silu(cz)

    q = rope_partial(aq.reshape(Bsz, L, N_HEADS, HEAD_DIM), positions)
    k = rope_partial(ak.reshape(Bsz, L, N_HEADS, HEAD_DIM), positions)
    v = av.reshape(Bsz, L, N_HEADS, HEAD_DIM)
    q_idx = rope_partial(iq.reshape(Bsz, L, N_IDX_HEADS, IDX_DIM), positions)
    k_idx = rope_partial(ik[:, :, None, :], positions)[:, :, 0, :]
    w_idx = iw * (N_IDX_HEADS ** -0.5)
    y_attn = dsa_sparse_attention(q, k, v, q_idx, k_idx, w_idx)
    y_attn = group_rmsnorm(y_attn, gn_attn, N_HEADS) * jax.nn.silu(az)

    y = jnp.concatenate([y_conv, y_attn], axis=-1) @ w_out
    return x + gate[:, None, :] * y


def setup_inputs(seed: int = 0) -> dict:
    key = jax.random.key(seed)
    ks = jax.random.split(key, 14)
    f32 = jnp.float32
    x = jax.random.normal(ks[0], (BATCH, SEQ, D_MODEL), f32)
    c = jax.random.normal(ks[1], (BATCH, D_MODEL), f32)
    offsets = jax.random.randint(ks[2], (BATCH, 1), 0, 1024, dtype=jnp.int32)
    positions = (jnp.arange(SEQ, dtype=jnp.int32)[None, :] + offsets).astype(jnp.int32)
    w_ada = jax.random.normal(ks[3], (D_MODEL, 3 * D_MODEL), f32) * (0.5 * D_MODEL ** -0.5)
    b_ada = jax.random.normal(ks[4], (3 * D_MODEL,), f32) * 0.1
    norm_in = 1.0 + 0.05 * jax.random.normal(ks[5], (D_MODEL,), f32)
    w_in = jax.random.normal(ks[6], (D_MODEL, D_IN), f32) * (D_MODEL ** -0.5)
    conv_w = jax.random.normal(ks[7], (CONV_WIDTH, D_CONV), f32) * (CONV_WIDTH ** -0.5)
    conv_b = jax.random.normal(ks[8], (D_CONV,), f32) * 0.02
    gn_conv = 1.0 + 0.05 * jax.random.normal(ks[9], (D_CONV,), f32)
    gn_attn = 1.0 + 0.05 * jax.random.normal(ks[10], (D_ATTN,), f32)
    w_out = jax.random.normal(ks[11], (D_MIX, D_MODEL), f32) * (D_MIX ** -0.5)
    norm_f = 1.0 + 0.05 * jax.random.normal(ks[12], (D_MODEL,), f32)
    return {"x": x, "c": c, "positions": positions, "w_ada": w_ada, "b_ada": b_ada,
            "norm_in": norm_in, "w_in": w_in, "conv_w": conv_w, "conv_b": conv_b,
            "gn_conv": gn_conv, "gn_attn": gn_attn, "w_out": w_out, "norm_f": norm_f}


def reference(x, c, positions, w_ada, b_ada, norm_in, w_in, conv_w, conv_b,
              gn_conv, gn_attn, w_out, norm_f):
    h = x
    for _ in range(DEPTH):
        h = hybrid_layer(h, c, positions, w_ada, b_ada, norm_in, w_in, conv_w, conv_b,
                         gn_conv, gn_attn, w_out)
    return rmsnorm(h, norm_f)
```

```python
import functools
import math

import jax
import jax.numpy as jnp
from jax import lax
from jax.experimental import pallas as pl
from jax.experimental.pallas import tpu as pltpu

D_MODEL = 1024
D_CONV = 512
D_ATTN = 512
HEAD_DIM = 64
N_HEADS = 8
CONV_GROUPS = 8
CONV_WIDTH = 3
ROPE_HALF = 8
ROPE_THETA = 500000.0
N_IDX_HEADS = 8
IDX_DIM = 64
TOPK = 256
EPS = 1e-6

LANES = 128
PAIR = 2 * HEAD_DIM
TS = 512
TQ = 256
TK = 256
NEG = -1e30
F32_MAX = float(jnp.finfo(jnp.float32).max)
INT_MIN = -(2 ** 31)
KEY_LOWEST_FINITE = INT_MIN + 0x00800000
VMEM_LIMIT = 56 * 1024 * 1024

F32 = jnp.float32
BF16 = jnp.bfloat16
I32 = jnp.int32


def _silu(v):
    return v * jax.nn.sigmoid(v)


def _ada_kernel(c_ref, w_ref, b_ref, o_ref):
    o_ref[...] = jnp.dot(_silu(c_ref[...]), w_ref[...], precision=lax.Precision.HIGHEST,
                         preferred_element_type=F32) + b_ref[...]


def _ada_call(c, w_ada, b_ada):
    bsz = c.shape[0]
    n_blk = 3
    return pl.pallas_call(
        _ada_kernel,
        grid=(n_blk,),
        in_specs=[pl.BlockSpec((bsz, D_MODEL), lambda n: (0, 0)),
                  pl.BlockSpec((D_MODEL, D_MODEL), lambda n: (0, n)),
                  pl.BlockSpec((1, D_MODEL), lambda n: (0, n))],
        out_specs=pl.BlockSpec((bsz, D_MODEL), lambda n: (0, n)),
        out_shape=jax.ShapeDtypeStruct((bsz, 3 * D_MODEL), F32),
        compiler_params=pltpu.CompilerParams(dimension_semantics=("arbitrary",)),
        name="adaln_mod",
    )(c, w_ada, b_ada.reshape(1, 3 * D_MODEL))


R_Q, R_K, R_V, R_QI, R_KI, R_W, R_END = 0, 512, 1024, 1536, 2048, 2176, 2192


def _rope_cm(xt, n_heads, cs, sn):
    t = xt.shape[-1]
    xr = xt.reshape(n_heads, HEAD_DIM, t)
    x1 = xr[:, 0:ROPE_HALF, :]
    x2 = xr[:, ROPE_HALF:2 * ROPE_HALF, :]
    r1 = x1 * cs - x2 * sn
    r2 = x1 * sn + x2 * cs
    return jnp.concatenate([r1, r2, xr[:, 2 * ROPE_HALF:, :]], axis=1).reshape(n_heads * HEAD_DIM, t)


def _pad_heads_to_pairs(xt):
    z = jnp.zeros((HEAD_DIM, xt.shape[-1]), xt.dtype)
    pieces = []
    for h in range(N_HEADS):
        blk = xt[h * HEAD_DIM:(h + 1) * HEAD_DIM, :]
        pieces += [blk, z] if h % 2 == 0 else [z, blk]
    return jnp.concatenate(pieces, axis=0)


def _proj_kernel(x_ref, pos_ref, mod_ref, normin_ref, wnat_ref, wcm_ref, convw_ref, convb_ref,
                 gnc_ref, gmat_ref,
                 yconv_ref, gz_ref, qt_ref, k_ref, vt_ref, qit_ref, ki_ref, wt_ref,
                 sbuf_ref):
    j = pl.program_id(1)
    x = x_ref[0]
    inv = lax.rsqrt(jnp.mean(x * x, axis=-1, keepdims=True) + EPS)
    mod = mod_ref[0]
    shift = mod[:, 0:D_MODEL]
    scale = mod[:, D_MODEL:2 * D_MODEL]
    h = (x * inv) * normin_ref[...] * (1.0 + scale) + shift
    hb = h.astype(BF16)

    nat = jnp.dot(hb, wnat_ref[...], preferred_element_type=F32)
    cb = nat[:, 0:512]
    cc = nat[:, 512:1024]
    cu = nat[:, 1024:1536]
    cz = nat[:, 1536:2048]
    az = nat[:, 2048:2560]
    gz_ref[0] = _silu(az)

    s = cc * cu

    @pl.when(j == 0)
    def _():
        sbuf_ref[0:8, :] = jnp.zeros((8, D_CONV), F32)

    sbuf_ref[8:TS + 8, :] = s
    s1 = sbuf_ref[7:TS + 7, :]
    s2 = sbuf_ref[6:TS + 6, :]
    cw = convw_ref[...]
    conv = convb_ref[...] + ((cw[0:1, :] * s2 + cw[1:2, :] * s1) + cw[2:3, :] * s)
    sbuf_ref[0:8, :] = sbuf_ref[TS:TS + 8, :]
    y = cb * conv
    ysq = y * y
    hi = ysq.astype(BF16)
    lo = (ysq - hi.astype(F32)).astype(BF16)
    gm = gmat_ref[...]
    ms = jnp.dot(hi, gm, preferred_element_type=F32) + jnp.dot(lo, gm, preferred_element_type=F32)
    yconv_ref[0] = ((y * lax.rsqrt(ms + EPS)) * gnc_ref[...] * _silu(cz)).astype(BF16)

    ht = h.T.astype(BF16)

    def proj_cm(r0, r1):
        return jnp.dot(wcm_ref[r0:r1, :], ht, preferred_element_type=F32)

    pos = pos_ref[0].astype(F32)
    jj = lax.broadcasted_iota(I32, (ROPE_HALF, 1), 0).astype(F32)
    inv_freq = jnp.exp(jj * (-math.log(ROPE_THETA) / ROPE_HALF))
    ang = inv_freq * pos
    cs = jnp.cos(ang)
    sn = jnp.sin(ang)

    scale_pow2 = HEAD_DIM ** -0.5
    q = _rope_cm(proj_cm(R_Q, R_K), N_HEADS, cs, sn) * scale_pow2
    qt_ref[0] = _pad_heads_to_pairs(q).astype(BF16)
    k = _rope_cm(proj_cm(R_K, R_V), N_HEADS, cs, sn)
    k_ref[0] = k.T.astype(BF16)
    vt_ref[0] = proj_cm(R_V, R_QI).astype(BF16)
    qi = _rope_cm(proj_cm(R_QI, R_KI), N_IDX_HEADS, cs, sn) * scale_pow2
    qit_ref[0] = _pad_heads_to_pairs(qi).astype(BF16)
    kiw = proj_cm(R_KI, R_END)
    ki2 = _rope_cm(kiw[0:PAIR, :], 2, cs, sn)
    ki_ref[0] = ki2.T.astype(BF16)
    wt_ref[0] = kiw[PAIR:PAIR + N_IDX_HEADS, :] * (N_IDX_HEADS ** -0.5)


def _proj_call(x, pos3, mod3, norm_in, w_nat, w_cm, conv_w, conv_b, gn_conv, gmat):
    bsz, seq, _ = x.shape
    const = lambda b, j: (0, 0)
    rows = lambda b, j: (b, j, 0)
    cols = lambda b, j: (b, 0, j)
    out_shapes = (
        jax.ShapeDtypeStruct((bsz, seq, D_CONV), BF16),
        jax.ShapeDtypeStruct((bsz, seq, D_ATTN), F32),
        jax.ShapeDtypeStruct((bsz, N_HEADS * PAIR, seq), BF16),
        jax.ShapeDtypeStruct((bsz, seq, D_ATTN), BF16),
        jax.ShapeDtypeStruct((bsz, D_ATTN, seq), BF16),
        jax.ShapeDtypeStruct((bsz, N_IDX_HEADS * PAIR, seq), BF16),
        jax.ShapeDtypeStruct((bsz, seq, PAIR), BF16),
        jax.ShapeDtypeStruct((bsz, N_IDX_HEADS, seq), F32),
    )
    return pl.pallas_call(
        _proj_kernel,
        grid=(bsz, seq // TS),
        in_specs=[
            pl.BlockSpec((1, TS, D_MODEL), rows),
            pl.BlockSpec((1, 1, TS), cols),
            pl.BlockSpec((1, 1, 3 * D_MODEL), lambda b, j: (b, 0, 0)),
            pl.BlockSpec((1, D_MODEL), const),
            pl.BlockSpec(w_nat.shape, const),
            pl.BlockSpec(w_cm.shape, const),
            pl.BlockSpec((CONV_WIDTH, D_CONV), const),
            pl.BlockSpec((1, D_CONV), const),
            pl.BlockSpec((1, D_CONV), const),
            pl.BlockSpec((D_CONV, D_CONV), const),
        ],
        out_specs=(
            pl.BlockSpec((1, TS, D_CONV), rows),
            pl.BlockSpec((1, TS, D_ATTN), rows),
            pl.BlockSpec((1, N_HEADS * PAIR, TS), cols),
            pl.BlockSpec((1, TS, D_ATTN), rows),
            pl.BlockSpec((1, D_ATTN, TS), cols),
            pl.BlockSpec((1, N_IDX_HEADS * PAIR, TS), cols),
            pl.BlockSpec((1, TS, PAIR), rows),
            pl.BlockSpec((1, N_IDX_HEADS, TS), cols),
        ),
        out_shape=out_shapes,
        scratch_shapes=[pltpu.VMEM((TS + 8, D_CONV), F32)],
        compiler_params=pltpu.CompilerParams(dimension_semantics=("arbitrary", "arbitrary"),
                                             vmem_limit_bytes=VMEM_LIMIT),
        name="in_proj",
    )(x, pos3, mod3, norm_in, w_nat, w_cm, conv_w, conv_b, gn_conv, gmat)


def _key_to_f32(key):
    bits = jnp.where(key < 0, key ^ 0x7FFFFFFF, key)
    return lax.bitcast_convert_type(bits, F32)


def _attn_kernel(qt_ref, qit_ref, wt_ref, k_ref, ki_ref, vt_ref, gz_ref, yconv_ref, x_ref, mod_ref,
                 gna_ref, wout_ref, normf_ref, o_ref, sc_ref, y_ref):
    i = pl.program_id(1)
    n_chunks = i + 1
    wt = wt_ref[0]
    row = lax.broadcasted_iota(I32, (TK, TQ), 0)
    col = lax.broadcasted_iota(I32, (TK, TQ), 1)

    def chunk_off(c):
        return pl.multiple_of(c * TK, TK)

    def score_chunk(c):
        kc = ki_ref[0, pl.ds(chunk_off(c), TK), :]
        acc = jnp.zeros((TK, TQ), F32)
        for hh in range(N_IDX_HEADS):
            d = jnp.dot(kc, qit_ref[0, hh * PAIR:(hh + 1) * PAIR, :], preferred_element_type=F32)
            acc = acc + wt[hh:hh + 1, :] * jnp.maximum(d, 0.0)
        return acc

    def score_body(c, carry):
        sc_ref[pl.ds(chunk_off(c), TK), :] = score_chunk(c)
        return carry

    lax.fori_loop(0, i, score_body, 0)
    sc_ref[pl.ds(chunk_off(i), TK), :] = jnp.where(row <= col, score_chunk(i), -jnp.inf)

    def count_chunks(pred):
        def body(c, acc):
            tile = sc_ref[pl.ds(chunk_off(c), TK), :]
            return acc + jnp.sum(pred(tile, c).reshape(TK // 8, 8, TQ), axis=0)
        acc = lax.fori_loop(0, n_chunks, body, jnp.zeros((8, TQ), I32))
        return jnp.sum(acc, axis=0, keepdims=True)

    def count_ge(thr):
        return count_chunks(lambda tile, c: (tile >= thr).astype(I32))

    cnt0 = count_ge(jnp.zeros((1, TQ), F32))
    ok0 = cnt0 >= TOPK
    lo0 = jnp.where(ok0, 0, INT_MIN).astype(I32)
    cnt_lo0 = jnp.where(ok0, cnt0, 0)

    def bit_body(b, carry):
        lo, cnt_lo = carry
        cand = lo | lax.shift_left(jnp.int32(1), 30 - b)
        cnt = count_ge(_key_to_f32(cand))
        ok = cnt >= TOPK
        return jnp.where(ok, cand, lo), jnp.where(ok, cnt, cnt_lo)

    lo, cnt_lo = lax.fori_loop(0, 31, bit_body, (lo0, cnt_lo0))
    has_kth = lo >= KEY_LOWEST_FINITE
    thr = jnp.where(has_kth, _key_to_f32(lo), -F32_MAX)
    tie = jnp.where(has_kth, cnt_lo, 0) > TOPK

    @pl.when(jnp.max(tie.astype(I32)) > 0)
    def _():
        need = TOPK - count_chunks(lambda tile, c: (tile > thr).astype(I32))

        def ties_before(p):
            def pred(tile, c):
                rid = row + c * TK
                return jnp.where(tile == thr, jnp.where(rid < p, 1, 0), 0).astype(I32)
            return count_chunks(pred)

        def idx_body(b, p):
            cand = p | lax.shift_left(jnp.int32(1), 11 - b)
            return jnp.where(ties_before(cand) < need, cand, p)

        p_last = lax.fori_loop(0, 12, idx_body, jnp.zeros((1, TQ), I32))
        p_last = jnp.where(tie, p_last, jnp.int32(2 ** 30))

        def drop_body(c, carry):
            tile = sc_ref[pl.ds(chunk_off(c), TK), :]
            rid = row + c * TK
            dropped = jnp.where(rid > p_last, -jnp.inf, tile)
            sc_ref[pl.ds(chunk_off(c), TK), :] = jnp.where(tile == thr, dropped, tile)
            return carry

        lax.fori_loop(0, n_chunks, drop_body, 0)

    def bias_body(c, carry):
        tile = sc_ref[pl.ds(chunk_off(c), TK), :]
        sc_ref[pl.ds(chunk_off(c), TK), :] = jnp.where(tile >= thr, 0.0, NEG)
        return carry

    lax.fori_loop(0, n_chunks, bias_body, 0)

    for h in range(N_HEADS):
        q2 = qt_ref[0, h * PAIR:(h + 1) * PAIR, :]
        lane0 = (h // 2) * PAIR

        def att_body(c, carry, q2=q2, lane0=lane0, h=h):
            m, l, acc = carry
            off = chunk_off(c)
            k2 = k_ref[0, pl.ds(off, TK), lane0:lane0 + PAIR]
            s = jnp.dot(k2, q2, preferred_element_type=F32) + sc_ref[pl.ds(off, TK), :]
            m_new = jnp.maximum(m, jnp.max(s, axis=0, keepdims=True))
            p = jnp.exp(s - m_new)
            alpha = jnp.exp(m - m_new)
            l_new = alpha * l + jnp.sum(p, axis=0, keepdims=True)
            vc = vt_ref[0, h * HEAD_DIM:(h + 1) * HEAD_DIM, pl.ds(off, TK)]
            acc_new = alpha * acc + jnp.dot(vc, p.astype(BF16), preferred_element_type=F32)
            return m_new, l_new, acc_new

        init = (jnp.full((1, TQ), NEG, F32), jnp.zeros((1, TQ), F32), jnp.zeros((HEAD_DIM, TQ), F32))
        _, l, acc = lax.fori_loop(0, n_chunks, att_body, init)
        out = acc / l
        ms = jnp.mean(out * out, axis=0, keepdims=True)
        y_ref[h * HEAD_DIM:(h + 1) * HEAD_DIM, :] = (
            (out * lax.rsqrt(ms + EPS)) * gna_ref[h * HEAD_DIM:(h + 1) * HEAD_DIM, :])

    ya = (y_ref[...].T * gz_ref[0]).astype(BF16)
    yo = (jnp.dot(yconv_ref[0], wout_ref[0:D_CONV, :], preferred_element_type=F32)
          + jnp.dot(ya, wout_ref[D_CONV:D_CONV + D_ATTN, :], preferred_element_type=F32))
    gate = mod_ref[0][:, 2 * D_MODEL:3 * D_MODEL]
    o = x_ref[0] + gate * yo
    inv = lax.rsqrt(jnp.mean(o * o, axis=-1, keepdims=True) + EPS)
    o_ref[0] = (o * inv) * normf_ref[...]


def _attn_call(qt, qit, wt, k, ki, vt, gz, yconv, x, mod3, gn_attn_col, w_out_bf, norm_f):
    bsz, seq, _ = x.shape
    const = lambda b, i: (0, 0)
    rows = lambda b, i: (b, i, 0)
    cols = lambda b, i: (b, 0, i)
    whole = lambda b, i: (b, 0, 0)
    return pl.pallas_call(
        _attn_kernel,
        grid=(bsz, seq // TQ),
        in_specs=[
            pl.BlockSpec((1, N_HEADS * PAIR, TQ), cols),
            pl.BlockSpec((1, N_IDX_HEADS * PAIR, TQ), cols),
            pl.BlockSpec((1, N_IDX_HEADS, TQ), cols),
            pl.BlockSpec((1, seq, D_ATTN), whole),
            pl.BlockSpec((1, seq, PAIR), whole),
            pl.BlockSpec((1, D_ATTN, seq), whole),
            pl.BlockSpec((1, TQ, D_ATTN), rows),
            pl.BlockSpec((1, TQ, D_CONV), rows),
            pl.BlockSpec((1, TQ, D_MODEL), rows),
            pl.BlockSpec((1, 1, 3 * D_MODEL), whole),
            pl.BlockSpec((D_ATTN, 1), const),
            pl.BlockSpec((D_CONV + D_ATTN, D_MODEL), const),
            pl.BlockSpec((1, D_MODEL), const),
        ],
        out_specs=pl.BlockSpec((1, TQ, D_MODEL), rows),
        out_shape=jax.ShapeDtypeStruct((bsz, seq, D_MODEL), F32),
        scratch_shapes=[pltpu.VMEM((seq, TQ), F32), pltpu.VMEM((D_ATTN, TQ), F32)],
        compiler_params=pltpu.CompilerParams(dimension_semantics=("arbitrary", "arbitrary"),
                                             vmem_limit_bytes=VMEM_LIMIT),
        name="dsa_attention",
    )(qt, qit, wt, k, ki, vt, gz, yconv, x, mod3, gn_attn_col, w_out_bf, norm_f)


def _split_in_proj(w_in):
    conv = w_in[:, 0:2048]
    aq, ak, av, az = (w_in[:, 2048 + n * 512:2560 + n * 512] for n in range(4))
    iq = w_in[:, 4096:4608]
    ik = w_in[:, 4608:4672]
    iw = w_in[:, 4672:4680]
    w_nat = jnp.concatenate([conv, az], axis=1).astype(BF16)
    pad = jnp.zeros((D_MODEL, R_END - R_W - N_IDX_HEADS), w_in.dtype)
    w_cm = jnp.concatenate([aq, ak, av, iq, ik, ik, iw, pad], axis=1).T.astype(BF16)
    return w_nat, w_cm


def kernel(x, c, positions, w_ada, b_ada, norm_in, w_in, conv_w, conv_b, gn_conv, gn_attn, w_out, norm_f):
    bsz, seq, _ = x.shape
    mod3 = _ada_call(c, w_ada, b_ada).reshape(bsz, 1, 3 * D_MODEL)
    w_nat, w_cm = _split_in_proj(w_in)
    group = jnp.arange(D_CONV) // (D_CONV // CONV_GROUPS)
    gmat = jnp.where(group[:, None] == group[None, :], 1.0 / (D_CONV // CONV_GROUPS), 0.0).astype(BF16)
    yconv, gz, qt, k, vt, qit, ki, wt = _proj_call(
        x, positions.reshape(bsz, 1, seq), mod3, norm_in.reshape(1, D_MODEL), w_nat, w_cm,
        conv_w, conv_b.reshape(1, D_CONV), gn_conv.reshape(1, D_CONV), gmat)
    return _attn_call(qt, qit, wt, k, ki, vt, gz, yconv, x, mod3, gn_attn.reshape(D_ATTN, 1),
                      w_out.astype(BF16), norm_f.reshape(1, D_MODEL))
```

```python
import functools
import math

import jax
import jax.numpy as jnp
from jax import lax
from jax.experimental import pallas as pl
from jax.experimental.pallas import tpu as pltpu

D_MODEL = 1024
D_CONV = 512
D_ATTN = 512
HEAD_DIM = 64
N_HEADS = 8
CONV_GROUPS = 8
CONV_WIDTH = 3
ROPE_HALF = 8
ROPE_THETA = 500000.0
N_IDX_HEADS = 8
IDX_DIM = 64
TOPK = 256
EPS = 1e-6

LANES = 128
PAIR = 2 * HEAD_DIM
TS = 512
TQ = 256
TK = 256
NEG = -1e30
F32_MAX = float(jnp.finfo(jnp.float32).max)
INT_MIN = -(2 ** 31)
KEY_LOWEST_FINITE = INT_MIN + 0x00800000
VMEM_LIMIT = 56 * 1024 * 1024

F32 = jnp.float32
BF16 = jnp.bfloat16
I32 = jnp.int32


def _silu(v):
    return v * jax.nn.sigmoid(v)


def _ada_kernel(c_ref, w_ref, b_ref, o_ref):
    o_ref[...] = jnp.dot(_silu(c_ref[...]), w_ref[...], precision=lax.Precision.HIGHEST,
                         preferred_element_type=F32) + b_ref[...]


def _ada_call(c, w_ada, b_ada):
    bsz = c.shape[0]
    n_blk = 3
    return pl.pallas_call(
        _ada_kernel,
        grid=(n_blk,),
        in_specs=[pl.BlockSpec((bsz, D_MODEL), lambda n: (0, 0)),
                  pl.BlockSpec((D_MODEL, D_MODEL), lambda n: (0, n)),
                  pl.BlockSpec((1, D_MODEL), lambda n: (0, n))],
        out_specs=pl.BlockSpec((bsz, D_MODEL), lambda n: (0, n)),
        out_shape=jax.ShapeDtypeStruct((bsz, 3 * D_MODEL), F32),
        compiler_params=pltpu.CompilerParams(dimension_semantics=("arbitrary",)),
        name="adaln_mod",
    )(c, w_ada, b_ada.reshape(1, 3 * D_MODEL))


R_Q, R_K, R_V, R_QI, R_KI, R_W, R_END = 0, 512, 1024, 1536, 2048, 2176, 2192


def _rope_cm(xt, n_heads, cs, sn):
    t = xt.shape[-1]
    xr = xt.reshape(n_heads, HEAD_DIM, t)
    x1 = xr[:, 0:ROPE_HALF, :]
    x2 = xr[:, ROPE_HALF:2 * ROPE_HALF, :]
    r1 = x1 * cs - x2 * sn
    r2 = x1 * sn + x2 * cs
    return jnp.concatenate([r1, r2, xr[:, 2 * ROPE_HALF:, :]], axis=1).reshape(n_heads * HEAD_DIM, t)


def _pad_heads_to_pairs(xt):
    z = jnp.zeros((HEAD_DIM, xt.shape[-1]), xt.dtype)
    pieces = []
    for h in range(N_HEADS):
        blk = xt[h * HEAD_DIM:(h + 1) * HEAD_DIM, :]
        pieces += [blk, z] if h % 2 == 0 else [z, blk]
    return jnp.concatenate(pieces, axis=0)


def _proj_kernel(x_ref, pos_ref, mod_ref, normin_ref, wnat_ref, wcm_ref, convw_ref, convb_ref,
                 gnc_ref, gmat_ref,
                 yconv_ref, gz_ref, qt_ref, k_ref, vt_ref, qit_ref, ki_ref, wt_ref,
                 sbuf_ref):
    j = pl.program_id(1)
    x = x_ref[0]
    inv = lax.rsqrt(jnp.mean(x * x, axis=-1, keepdims=True) + EPS)
    mod = mod_ref[0]
    shift = mod[:, 0:D_MODEL]
    scale = mod[:, D_MODEL:2 * D_MODEL]
    h = (x * inv) * normin_ref[...] * (1.0 + scale) + shift
    hb = h.astype(BF16)

    nat = jnp.dot(hb, wnat_ref[...], preferred_element_type=F32)
    cb = nat[:, 0:512]
    cc = nat[:, 512:1024]
    cu = nat[:, 1024:1536]
    cz = nat[:, 1536:2048]
    az = nat[:, 2048:2560]
    gz_ref[0] = _silu(az)

    s = cc * cu

    @pl.when(j == 0)
    def _():
        sbuf_ref[0:8, :] = jnp.zeros((8, D_CONV), F32)

    sbuf_ref[8:TS + 8, :] = s
    s1 = sbuf_ref[7:TS + 7, :]
    s2 = sbuf_ref[6:TS + 6, :]
    cw = convw_ref[...]
    conv = convb_ref[...] + ((cw[0:1, :] * s2 + cw[1:2, :] * s1) + cw[2:3, :] * s)
    sbuf_ref[0:8, :] = sbuf_ref[TS:TS + 8, :]
    y = cb * conv
    ysq = y * y
    hi = ysq.astype(BF16)
    lo = (ysq - hi.astype(F32)).astype(BF16)
    gm = gmat_ref[...]
    ms = jnp.dot(hi, gm, preferred_element_type=F32) + jnp.dot(lo, gm, preferred_element_type=F32)
    yconv_ref[0] = ((y * lax.rsqrt(ms + EPS)) * gnc_ref[...] * _silu(cz)).astype(BF16)

    ht = h.T.astype(BF16)

    def proj_cm(r0, r1):
        return jnp.dot(wcm_ref[r0:r1, :], ht, preferred_element_type=F32)

    pos = pos_ref[0].astype(F32)
    jj = lax.broadcasted_iota(I32, (ROPE_HALF, 1), 0).astype(F32)
    inv_freq = jnp.exp(jj * (-math.log(ROPE_THETA) / ROPE_HALF))
    ang = inv_freq * pos
    cs = jnp.cos(ang)
    sn = jnp.sin(ang)

    scale_pow2 = HEAD_DIM ** -0.5
    q = _rope_cm(proj_cm(R_Q, R_K), N_HEADS, cs, sn) * scale_pow2
    qt_ref[0] = _pad_heads_to_pairs(q).astype(BF16)
    k = _rope_cm(proj_cm(R_K, R_V), N_HEADS, cs, sn)
    k_ref[0] = k.T.astype(BF16)
    vt_ref[0] = proj_cm(R_V, R_QI).astype(BF16)
    qi = _rope_cm(proj_cm(R_QI, R_KI), N_IDX_HEADS, cs, sn) * scale_pow2
    qit_ref[0] = _pad_heads_to_pairs(qi).astype(BF16)
    kiw = proj_cm(R_KI, R_END)
    ki2 = _rope_cm(kiw[0:PAIR, :], 2, cs, sn)
    ki_ref[0] = ki2.T.astype(BF16)
    wt_ref[0] = kiw[PAIR:PAIR + N_IDX_HEADS, :] * (N_IDX_HEADS ** -0.5)


def _proj_call(x, pos3, mod3, norm_in, w_nat, w_cm, conv_w, conv_b, gn_conv, gmat):
    bsz, seq, _ = x.shape
    const = lambda b, j: (0, 0)
    rows = lambda b, j: (b, j, 0)
    cols = lambda b, j: (b, 0, j)
    out_shapes = (
        jax.ShapeDtypeStruct((bsz, seq, D_CONV), BF16),
        jax.ShapeDtypeStruct((bsz, seq, D_ATTN), F32),
        jax.ShapeDtypeStruct((bsz, N_HEADS * PAIR, seq), BF16),
        jax.ShapeDtypeStruct((bsz, seq, D_ATTN), BF16),
        jax.ShapeDtypeStruct((bsz, D_ATTN, seq), BF16),
        jax.ShapeDtypeStruct((bsz, N_IDX_HEADS * PAIR, seq), BF16),
        jax.ShapeDtypeStruct((bsz, seq, PAIR), BF16),
        jax.ShapeDtypeStruct((bsz, N_IDX_HEADS, seq), F32),
    )
    return pl.pallas_call(
        _proj_kernel,
        grid=(bsz, seq // TS),
        in_specs=[
            pl.BlockSpec((1, TS, D_MODEL), rows),
            pl.BlockSpec((1, 1, TS), cols),
            pl.BlockSpec((1, 1, 3 * D_MODEL), lambda b, j: (b, 0, 0)),
            pl.BlockSpec((1, D_MODEL), const),
            pl.BlockSpec(w_nat.shape, const),
            pl.BlockSpec(w_cm.shape, const),
            pl.BlockSpec((CONV_WIDTH, D_CONV), const),
            pl.BlockSpec((1, D_CONV), const),
            pl.BlockSpec((1, D_CONV), const),
            pl.BlockSpec((D_CONV, D_CONV), const),
        ],
        out_specs=(
            pl.BlockSpec((1, TS, D_CONV), rows),
            pl.BlockSpec((1, TS, D_ATTN), rows),
            pl.BlockSpec((1, N_HEADS * PAIR, TS), cols),
            pl.BlockSpec((1, TS, D_ATTN), rows),
            pl.BlockSpec((1, D_ATTN, TS), cols),
            pl.BlockSpec((1, N_IDX_HEADS * PAIR, TS), cols),
            pl.BlockSpec((1, TS, PAIR), rows),
            pl.BlockSpec((1, N_IDX_HEADS, TS), cols),
        ),
        out_shape=out_shapes,
        scratch_shapes=[pltpu.VMEM((TS + 8, D_CONV), F32)],
        compiler_params=pltpu.CompilerParams(dimension_semantics=("arbitrary", "arbitrary"),
                                             vmem_limit_bytes=VMEM_LIMIT),
        name="in_proj",
    )(x, pos3, mod3, norm_in, w_nat, w_cm, conv_w, conv_b, gn_conv, gmat)


def _key_to_f32(key):
    bits = jnp.where(key < 0, key ^ 0x7FFFFFFF, key)
    return lax.bitcast_convert_type(bits, F32)


def _attn_kernel(qt_ref, qit_ref, wt_ref, k_ref, ki_ref, vt_ref, gz_ref, yconv_ref, x_ref, mod_ref,
                 gna_ref, wout_ref, normf_ref, o_ref, sc_ref, y_ref, s8_ref):
    i = pl.program_id(1)
    n_chunks = i + 1
    wt = wt_ref[0]
    row = lax.broadcasted_iota(I32, (TK, TQ), 0)
    col = lax.broadcasted_iota(I32, (TK, TQ), 1)

    def chunk_off(c):
        return pl.multiple_of(c * TK, TK)

    def score_chunk(c):
        kc = ki_ref[0, pl.ds(chunk_off(c), TK), :]
        acc = jnp.zeros((TK, TQ), F32)
        for hh in range(N_IDX_HEADS):
            d = jnp.dot(kc, qit_ref[0, hh * PAIR:(hh + 1) * PAIR, :], preferred_element_type=F32)
            acc = acc + wt[hh:hh + 1, :] * jnp.maximum(d, 0.0)
        return acc

    def score_body(c, carry):
        sc_ref[pl.ds(chunk_off(c), TK), :] = score_chunk(c)
        return carry

    lax.fori_loop(0, i, score_body, 0)
    sc_ref[pl.ds(chunk_off(i), TK), :] = jnp.where(row <= col, score_chunk(i), -jnp.inf)

    def count_chunks(pred):
        def body(c, acc):
            tile = sc_ref[pl.ds(chunk_off(c), TK), :]
            return acc + jnp.sum(pred(tile, c).reshape(TK // 8, 8, TQ), axis=0)
        acc = lax.fori_loop(0, n_chunks, body, jnp.zeros((8, TQ), I32))
        return jnp.sum(acc, axis=0, keepdims=True)

    def count_ge(thr):
        return count_chunks(lambda tile, c: (tile >= thr).astype(I32))

    cnt0 = count_ge(jnp.zeros((1, TQ), F32))
    ok0 = cnt0 >= TOPK
    lo0 = jnp.where(ok0, 0, INT_MIN).astype(I32)
    cnt_lo0 = jnp.where(ok0, cnt0, 0)

    def bit_body(b, carry):
        lo, cnt_lo = carry
        cand = lo | lax.shift_left(jnp.int32(1), 30 - b)
        cnt = count_ge(_key_to_f32(cand))
        ok = cnt >= TOPK
        return jnp.where(ok, cand, lo), jnp.where(ok, cnt, cnt_lo)

    lo, cnt_lo = lax.fori_loop(0, 31, bit_body, (lo0, cnt_lo0))
    has_kth = lo >= KEY_LOWEST_FINITE
    thr = jnp.where(has_kth, _key_to_f32(lo), -F32_MAX)
    tie = jnp.where(has_kth, cnt_lo, 0) > TOPK

    @pl.when(jnp.max(tie.astype(I32)) > 0)
    def _():
        need = TOPK - count_chunks(lambda tile, c: (tile > thr).astype(I32))

        def ties_before(p):
            def pred(tile, c):
                rid = row + c * TK
                return jnp.where(tile == thr, jnp.where(rid < p, 1, 0), 0).astype(I32)
            return count_chunks(pred)

        def idx_body(b, p):
            cand = p | lax.shift_left(jnp.int32(1), 11 - b)
            return jnp.where(ties_before(cand) < need, cand, p)

        p_last = lax.fori_loop(0, 12, idx_body, jnp.zeros((1, TQ), I32))
        p_last = jnp.where(tie, p_last, jnp.int32(2 ** 30))

        def drop_body(c, carry):
            tile = sc_ref[pl.ds(chunk_off(c), TK), :]
            rid = row + c * TK
            dropped = jnp.where(rid > p_last, -jnp.inf, tile)
            sc_ref[pl.ds(chunk_off(c), TK), :] = jnp.where(tile == thr, dropped, tile)
            return carry

        lax.fori_loop(0, n_chunks, drop_body, 0)

    def bias_body(c, carry):
        tile = sc_ref[pl.ds(chunk_off(c), TK), :]
        sc_ref[pl.ds(chunk_off(c), TK), :] = jnp.where(tile >= thr, 0.0, NEG)
        return carry

    lax.fori_loop(0, n_chunks, bias_body, 0)

    y_ref[...] = jnp.zeros((D_ATTN, TQ), F32)

    def att_body(c, carry):
        ms, ls = carry
        off = chunk_off(c)
        new_ms, new_ls = [], []

        def logits(h):
            lane0 = (h // 2) * PAIR
            q2 = qt_ref[0, h * PAIR:(h + 1) * PAIR, :]
            k2 = k_ref[0, pl.ds(off, TK), lane0:lane0 + PAIR]
            return jnp.dot(k2, q2, preferred_element_type=F32)

        bias = sc_ref[pl.ds(off, TK), :]
        for h in range(N_HEADS):
            s8_ref[h] = logits(h) + bias
        for h in range(N_HEADS):
            rows_h = slice(h * HEAD_DIM, (h + 1) * HEAD_DIM)
            s = s8_ref[h]
            m_new = jnp.maximum(ms[h], jnp.max(s, axis=0, keepdims=True))
            p = jnp.exp(s - m_new)
            alpha = jnp.exp(ms[h] - m_new)
            new_ms.append(m_new)
            new_ls.append(alpha * ls[h] + jnp.sum(p, axis=0, keepdims=True))
            vc = vt_ref[0, rows_h, pl.ds(off, TK)]
            y_ref[rows_h, :] = alpha * y_ref[rows_h, :] + jnp.dot(
                vc, p.astype(BF16), preferred_element_type=F32)
        return tuple(new_ms), tuple(new_ls)

    init = (tuple(jnp.full((1, TQ), NEG, F32) for _ in range(N_HEADS)),
            tuple(jnp.zeros((1, TQ), F32) for _ in range(N_HEADS)))
    _, ls = lax.fori_loop(0, n_chunks, att_body, init)
    for h in range(N_HEADS):
        rows_h = slice(h * HEAD_DIM, (h + 1) * HEAD_DIM)
        out = y_ref[rows_h, :] / ls[h]
        ms_h = jnp.mean(out * out, axis=0, keepdims=True)
        y_ref[rows_h, :] = (out * lax.rsqrt(ms_h + EPS)) * gna_ref[rows_h, :]

    ya = (y_ref[...].T * gz_ref[0]).astype(BF16)
    yo = (jnp.dot(yconv_ref[0], wout_ref[0:D_CONV, :], preferred_element_type=F32)
          + jnp.dot(ya, wout_ref[D_CONV:D_CONV + D_ATTN, :], preferred_element_type=F32))
    gate = mod_ref[0][:, 2 * D_MODEL:3 * D_MODEL]
    o = x_ref[0] + gate * yo
    inv = lax.rsqrt(jnp.mean(o * o, axis=-1, keepdims=True) + EPS)
    o_ref[0] = (o * inv) * normf_ref[...]


def _attn_call(qt, qit, wt, k, ki, vt, gz, yconv, x, mod3, gn_attn_col, w_out_bf, norm_f):
    bsz, seq, _ = x.shape
    const = lambda b, i: (0, 0)
    rows = lambda b, i: (b, i, 0)
    cols = lambda b, i: (b, 0, i)
    whole = lambda b, i: (b, 0, 0)
    return pl.pallas_call(
        _attn_kernel,
        grid=(bsz, seq // TQ),
        in_specs=[
            pl.BlockSpec((1, N_HEADS * PAIR, TQ), cols),
            pl.BlockSpec((1, N_IDX_HEADS * PAIR, TQ), cols),
            pl.BlockSpec((1, N_IDX_HEADS, TQ), cols),
            pl.BlockSpec((1, seq, D_ATTN), whole),
            pl.BlockSpec((1, seq, PAIR), whole),
            pl.BlockSpec((1, D_ATTN, seq), whole),
            pl.BlockSpec((1, TQ, D_ATTN), rows),
            pl.BlockSpec((1, TQ, D_CONV), rows),
            pl.BlockSpec((1, TQ, D_MODEL), rows),
            pl.BlockSpec((1, 1, 3 * D_MODEL), whole),
            pl.BlockSpec((D_ATTN, 1), const),
            pl.BlockSpec((D_CONV + D_ATTN, D_MODEL), const),
            pl.BlockSpec((1, D_MODEL), const),
        ],
        out_specs=pl.BlockSpec((1, TQ, D_MODEL), rows),
        out_shape=jax.ShapeDtypeStruct((bsz, seq, D_MODEL), F32),
        scratch_shapes=[pltpu.VMEM((seq, TQ), F32), pltpu.VMEM((D_ATTN, TQ), F32),
                        pltpu.VMEM((N_HEADS, TK, TQ), F32)],
        compiler_params=pltpu.CompilerParams(dimension_semantics=("arbitrary", "arbitrary"),
                                             vmem_limit_bytes=VMEM_LIMIT),
        name="dsa_attention",
    )(qt, qit, wt, k, ki, vt, gz, yconv, x, mod3, gn_attn_col, w_out_bf, norm_f)


def _split_in_proj(w_in):
    conv = w_in[:, 0:2048]
    aq, ak, av, az = (w_in[:, 2048 + n * 512:2560 + n * 512] for n in range(4))
    iq = w_in[:, 4096:4608]
    ik = w_in[:, 4608:4672]
    iw = w_in[:, 4672:4680]
    w_nat = jnp.concatenate([conv, az], axis=1).astype(BF16)
    pad = jnp.zeros((D_MODEL, R_END - R_W - N_IDX_HEADS), w_in.dtype)
    w_cm = jnp.concatenate([aq, ak, av, iq, ik, ik, iw, pad], axis=1).T.astype(BF16)
    return w_nat, w_cm


def kernel(x, c, positions, w_ada, b_ada, norm_in, w_in, conv_w, conv_b, gn_conv, gn_attn, w_out, norm_f):
    bsz, seq, _ = x.shape
    mod3 = _ada_call(c, w_ada, b_ada).reshape(bsz, 1, 3 * D_MODEL)
    w_nat, w_cm = _split_in_proj(w_in)
    group = jnp.arange(D_CONV) // (D_CONV // CONV_GROUPS)
    gmat = jnp.where(group[:, None] == group[None, :], 1.0 / (D_CONV // CONV_GROUPS), 0.0).astype(BF16)
    yconv, gz, qt, k, vt, qit, ki, wt = _proj_call(
        x, positions.reshape(bsz, 1, seq), mod3, norm_in.reshape(1, D_MODEL), w_nat, w_cm,
        conv_w, conv_b.reshape(1, D_CONV), gn_conv.reshape(1, D_CONV), gmat)
    return _attn_call(qt, qit, wt, k, ki, vt, gz, yconv, x, mod3, gn_attn.reshape(D_ATTN, 1),
                      w_out.astype(BF16), norm_f.reshape(1, D_MODEL))
```

```python
import functools
import math

import jax
import jax.numpy as jnp
from jax import lax
from jax.experimental import pallas as pl
from jax.experimental.pallas import tpu as pltpu

D_MODEL = 1024
D_CONV = 512
D_ATTN = 512
HEAD_DIM = 64
N_HEADS = 8
CONV_GROUPS = 8
CONV_WIDTH = 3
ROPE_HALF = 8
ROPE_THETA = 500000.0
N_IDX_HEADS = 8
IDX_DIM = 64
TOPK = 256
EPS = 1e-6

LANES = 128
PAIR = 2 * HEAD_DIM
TS = 512
TQ = 512
TK = 512
NEG = -1e30
F32_MAX = float(jnp.finfo(jnp.float32).max)
INT_MIN = -(2 ** 31)
KEY_LOWEST_FINITE = INT_MIN + 0x00800000
VMEM_LIMIT = 56 * 1024 * 1024

F32 = jnp.float32
BF16 = jnp.bfloat16
I32 = jnp.int32


def _silu(v):
    return v * jax.nn.sigmoid(v)


def _ada_kernel(c_ref, w_ref, b_ref, o_ref):
    o_ref[...] = jnp.dot(_silu(c_ref[...]), w_ref[...], precision=lax.Precision.HIGHEST,
                         preferred_element_type=F32) + b_ref[...]


def _ada_call(c, w_ada, b_ada):
    bsz = c.shape[0]
    n_blk = 3
    return pl.pallas_call(
        _ada_kernel,
        grid=(n_blk,),
        in_specs=[pl.BlockSpec((bsz, D_MODEL), lambda n: (0, 0)),
                  pl.BlockSpec((D_MODEL, D_MODEL), lambda n: (0, n)),
                  pl.BlockSpec((1, D_MODEL), lambda n: (0, n))],
        out_specs=pl.BlockSpec((bsz, D_MODEL), lambda n: (0, n)),
        out_shape=jax.ShapeDtypeStruct((bsz, 3 * D_MODEL), F32),
        compiler_params=pltpu.CompilerParams(dimension_semantics=("arbitrary",)),
        name="adaln_mod",
    )(c, w_ada, b_ada.reshape(1, 3 * D_MODEL))


R_Q, R_K, R_V, R_QI, R_KI, R_W, R_END = 0, 512, 1024, 1536, 2048, 2176, 2192


def _rope_cm(xt, n_heads, cs, sn):
    t = xt.shape[-1]
    xr = xt.reshape(n_heads, HEAD_DIM, t)
    x1 = xr[:, 0:ROPE_HALF, :]
    x2 = xr[:, ROPE_HALF:2 * ROPE_HALF, :]
    r1 = x1 * cs - x2 * sn
    r2 = x1 * sn + x2 * cs
    return jnp.concatenate([r1, r2, xr[:, 2 * ROPE_HALF:, :]], axis=1).reshape(n_heads * HEAD_DIM, t)


def _pad_heads_to_pairs(xt):
    z = jnp.zeros((HEAD_DIM, xt.shape[-1]), xt.dtype)
    pieces = []
    for h in range(N_HEADS):
        blk = xt[h * HEAD_DIM:(h + 1) * HEAD_DIM, :]
        pieces += [blk, z] if h % 2 == 0 else [z, blk]
    return jnp.concatenate(pieces, axis=0)


def _proj_kernel(x_ref, pos_ref, mod_ref, normin_ref, wnat_ref, wcm_ref, convw_ref, convb_ref,
                 gnc_ref, gmat_ref,
                 yconv_ref, gz_ref, qt_ref, k_ref, vt_ref, qit_ref, ki_ref, wt_ref,
                 sbuf_ref):
    j = pl.program_id(1)
    x = x_ref[0]
    inv = lax.rsqrt(jnp.mean(x * x, axis=-1, keepdims=True) + EPS)
    mod = mod_ref[0]
    shift = mod[:, 0:D_MODEL]
    scale = mod[:, D_MODEL:2 * D_MODEL]
    h = (x * inv) * normin_ref[...] * (1.0 + scale) + shift
    hb = h.astype(BF16)

    nat = jnp.dot(hb, wnat_ref[...], preferred_element_type=F32)
    cb = nat[:, 0:512]
    cc = nat[:, 512:1024]
    cu = nat[:, 1024:1536]
    cz = nat[:, 1536:2048]
    az = nat[:, 2048:2560]
    gz_ref[0] = _silu(az)

    s = cc * cu

    @pl.when(j == 0)
    def _():
        sbuf_ref[0:8, :] = jnp.zeros((8, D_CONV), F32)

    sbuf_ref[8:TS + 8, :] = s
    s1 = sbuf_ref[7:TS + 7, :]
    s2 = sbuf_ref[6:TS + 6, :]
    cw = convw_ref[...]
    conv = convb_ref[...] + ((cw[0:1, :] * s2 + cw[1:2, :] * s1) + cw[2:3, :] * s)
    sbuf_ref[0:8, :] = sbuf_ref[TS:TS + 8, :]
    y = cb * conv
    ysq = y * y
    hi = ysq.astype(BF16)
    lo = (ysq - hi.astype(F32)).astype(BF16)
    gm = gmat_ref[...]
    ms = jnp.dot(hi, gm, preferred_element_type=F32) + jnp.dot(lo, gm, preferred_element_type=F32)
    yconv_ref[0] = ((y * lax.rsqrt(ms + EPS)) * gnc_ref[...] * _silu(cz)).astype(BF16)

    ht = h.T.astype(BF16)

    def proj_cm(r0, r1):
        return jnp.dot(wcm_ref[r0:r1, :], ht, preferred_element_type=F32)

    pos = pos_ref[0].astype(F32)
    jj = lax.broadcasted_iota(I32, (ROPE_HALF, 1), 0).astype(F32)
    theta = jnp.full((ROPE_HALF, 1), ROPE_THETA, F32)
    inv_freq = jnp.exp(jnp.log(theta) * (jj * (-1.0 / ROPE_HALF)))
    ang = inv_freq * pos
    cs = jnp.cos(ang)
    sn = jnp.sin(ang)

    scale_pow2 = HEAD_DIM ** -0.5
    q = _rope_cm(proj_cm(R_Q, R_K), N_HEADS, cs, sn) * scale_pow2
    qt_ref[0] = _pad_heads_to_pairs(q).astype(BF16)
    k = _rope_cm(proj_cm(R_K, R_V), N_HEADS, cs, sn)
    k_ref[0] = k.T.astype(BF16)
    vt_ref[0] = proj_cm(R_V, R_QI).astype(BF16)
    qi = _rope_cm(proj_cm(R_QI, R_KI), N_IDX_HEADS, cs, sn) * scale_pow2
    qit_ref[0] = _pad_heads_to_pairs(qi).astype(BF16)
    kiw = proj_cm(R_KI, R_END)
    ki2 = _rope_cm(kiw[0:PAIR, :], 2, cs, sn)
    ki_ref[0] = ki2.T.astype(BF16)
    wt_ref[0] = kiw[PAIR:PAIR + N_IDX_HEADS, :] * (N_IDX_HEADS ** -0.5)


def _proj_call(x, pos3, mod3, norm_in, w_nat, w_cm, conv_w, conv_b, gn_conv, gmat):
    bsz, seq, _ = x.shape
    const = lambda b, j: (0, 0)
    rows = lambda b, j: (b, j, 0)
    cols = lambda b, j: (b, 0, j)
    out_shapes = (
        jax.ShapeDtypeStruct((bsz, seq, D_CONV), BF16),
        jax.ShapeDtypeStruct((bsz, seq, D_ATTN), F32),
        jax.ShapeDtypeStruct((bsz, N_HEADS * PAIR, seq), BF16),
        jax.ShapeDtypeStruct((bsz, seq, D_ATTN), BF16),
        jax.ShapeDtypeStruct((bsz, D_ATTN, seq), BF16),
        jax.ShapeDtypeStruct((bsz, N_IDX_HEADS * PAIR, seq), BF16),
        jax.ShapeDtypeStruct((bsz, seq, PAIR), BF16),
        jax.ShapeDtypeStruct((bsz, N_IDX_HEADS, seq), F32),
    )
    return pl.pallas_call(
        _proj_kernel,
        grid=(bsz, seq // TS),
        in_specs=[
            pl.BlockSpec((1, TS, D_MODEL), rows),
            pl.BlockSpec((1, 1, TS), cols),
            pl.BlockSpec((1, 1, 3 * D_MODEL), lambda b, j: (b, 0, 0)),
            pl.BlockSpec((1, D_MODEL), const),
            pl.BlockSpec(w_nat.shape, const),
            pl.BlockSpec(w_cm.shape, const),
            pl.BlockSpec((CONV_WIDTH, D_CONV), const),
            pl.BlockSpec((1, D_CONV), const),
            pl.BlockSpec((1, D_CONV), const),
            pl.BlockSpec((D_CONV, D_CONV), const),
        ],
        out_specs=(
            pl.BlockSpec((1, TS, D_CONV), rows),
            pl.BlockSpec((1, TS, D_ATTN), rows),
            pl.BlockSpec((1, N_HEADS * PAIR, TS), cols),
            pl.BlockSpec((1, TS, D_ATTN), rows),
            pl.BlockSpec((1, D_ATTN, TS), cols),
            pl.BlockSpec((1, N_IDX_HEADS * PAIR, TS), cols),
            pl.BlockSpec((1, TS, PAIR), rows),
            pl.BlockSpec((1, N_IDX_HEADS, TS), cols),
        ),
        out_shape=out_shapes,
        scratch_shapes=[pltpu.VMEM((TS + 8, D_CONV), F32)],
        compiler_params=pltpu.CompilerParams(dimension_semantics=("arbitrary", "arbitrary"),
                                             vmem_limit_bytes=VMEM_LIMIT),
        name="in_proj",
    )(x, pos3, mod3, norm_in, w_nat, w_cm, conv_w, conv_b, gn_conv, gmat)


def _key_to_f32(key):
    bits = jnp.where(key < 0, key ^ 0x7FFFFFFF, key)
    return lax.bitcast_convert_type(bits, F32)


def _attn_kernel(qt_ref, qit_ref, wt_ref, k_ref, ki_ref, vt_ref, gz_ref, yconv_ref, x_ref, mod_ref,
                 gna_ref, wout_ref, normf_ref, o_ref, sc_ref, y_ref, s8_ref, thr_ref):
    i = pl.program_id(1)
    n_chunks = i + 1
    wt = wt_ref[0]
    row = lax.broadcasted_iota(I32, (TK, TQ), 0)
    col = lax.broadcasted_iota(I32, (TK, TQ), 1)

    def chunk_off(c):
        return pl.multiple_of(c * TK, TK)

    def score_chunk(c):
        kc = ki_ref[0, pl.ds(chunk_off(c), TK), :]
        acc = jnp.zeros((TK, TQ), F32)
        for hh in range(N_IDX_HEADS):
            d = jnp.dot(kc, qit_ref[0, hh * PAIR:(hh + 1) * PAIR, :], preferred_element_type=F32)
            acc = acc + wt[hh:hh + 1, :] * jnp.maximum(d, 0.0)
        return acc

    def score_body(c, carry):
        sc_ref[pl.ds(chunk_off(c), TK), :] = score_chunk(c)
        return carry

    lax.fori_loop(0, i, score_body, 0)
    sc_ref[pl.ds(chunk_off(i), TK), :] = jnp.where(row <= col, score_chunk(i), -jnp.inf)

    def count_chunks(pred):
        def body(c, acc):
            tile = sc_ref[pl.ds(chunk_off(c), TK), :]
            return acc + jnp.sum(pred(tile, c).reshape(TK // 8, 8, TQ), axis=0)
        acc = lax.fori_loop(0, n_chunks, body, jnp.zeros((8, TQ), I32))
        return jnp.sum(acc, axis=0, keepdims=True)

    def count_ge(thr):
        return count_chunks(lambda tile, c: (tile >= thr).astype(I32))

    cnt0 = count_ge(jnp.zeros((1, TQ), F32))
    ok0 = cnt0 >= TOPK
    lo0 = jnp.where(ok0, 0, INT_MIN).astype(I32)
    cnt_lo0 = jnp.where(ok0, cnt0, 0)

    def bit_body(b, carry):
        lo, cnt_lo = carry
        cand = lo | lax.shift_left(jnp.int32(1), 30 - b)
        cnt = count_ge(_key_to_f32(cand))
        ok = cnt >= TOPK
        return jnp.where(ok, cand, lo), jnp.where(ok, cnt, cnt_lo)

    lo, cnt_lo = lax.fori_loop(0, 31, bit_body, (lo0, cnt_lo0))
    has_kth = lo >= KEY_LOWEST_FINITE
    thr0 = jnp.where(has_kth, _key_to_f32(lo), -F32_MAX)
    tie0 = jnp.where(has_kth, cnt_lo, 0) > TOPK

    thr_ref[...] = thr0

    @pl.when(jnp.max(tie0.astype(I32)) > 0)
    def _():
        def midpoint(lo_f, hi_f):
            return lo_f + (hi_f - lo_f) * 0.5

        def splits(lo_f, hi_f):
            mid = midpoint(lo_f, hi_f)
            return jnp.where(mid > lo_f, jnp.where(mid < hi_f, 1, 0), 0).astype(I32)

        def refine_cond(carry):
            lo_f, hi_f, it = carry
            return jnp.logical_and(jnp.max(splits(lo_f, hi_f)) > 0, it < 64)

        def refine_body(carry):
            lo_f, hi_f, it = carry
            mid = midpoint(lo_f, hi_f)
            act = splits(lo_f, hi_f) > 0
            ok = count_ge(mid) >= TOPK
            return (jnp.where(act, jnp.where(ok, mid, lo_f), lo_f),
                    jnp.where(act, jnp.where(ok, hi_f, mid), hi_f), it + 1)

        above0 = jnp.where(tie0, _key_to_f32(lo + 1), thr0)
        thr, _, _ = lax.while_loop(refine_cond, refine_body, (thr0, above0, jnp.int32(0)))
        thr_ref[...] = thr

        tie = jnp.where(tie0, count_ge(thr), 0) > TOPK
        need = TOPK - count_chunks(lambda tile, c: (tile > thr).astype(I32))

        def ties_before(p):
            def pred(tile, c):
                rid = row + c * TK
                return jnp.where(tile == thr, jnp.where(rid < p, 1, 0), 0).astype(I32)
            return count_chunks(pred)

        def idx_body(b, p):
            cand = p | lax.shift_left(jnp.int32(1), 11 - b)
            return jnp.where(ties_before(cand) < need, cand, p)

        p_last = lax.fori_loop(0, 12, idx_body, jnp.zeros((1, TQ), I32))
        p_last = jnp.where(tie, p_last, jnp.int32(2 ** 30))

        def drop_body(c, carry):
            tile = sc_ref[pl.ds(chunk_off(c), TK), :]
            rid = row + c * TK
            dropped = jnp.where(rid > p_last, -jnp.inf, tile)
            sc_ref[pl.ds(chunk_off(c), TK), :] = jnp.where(tile == thr, dropped, tile)
            return carry

        lax.fori_loop(0, n_chunks, drop_body, 0)

    thr = thr_ref[...]

    def bias_body(c, carry):
        tile = sc_ref[pl.ds(chunk_off(c), TK), :]
        sc_ref[pl.ds(chunk_off(c), TK), :] = jnp.where(tile >= thr, 0.0, NEG)
        return carry

    lax.fori_loop(0, n_chunks, bias_body, 0)

    y_ref[...] = jnp.zeros((D_ATTN, TQ), F32)

    def att_body(c, carry):
        ms, ls = carry
        off = chunk_off(c)
        new_ms, new_ls = [], []

        def logits(h):
            lane0 = (h // 2) * PAIR
            q2 = qt_ref[0, h * PAIR:(h + 1) * PAIR, :]
            k2 = k_ref[0, pl.ds(off, TK), lane0:lane0 + PAIR]
            return jnp.dot(k2, q2, preferred_element_type=F32)

        bias = sc_ref[pl.ds(off, TK), :]
        for h in range(N_HEADS):
            s8_ref[h] = logits(h) + bias
        for h in range(N_HEADS):
            rows_h = slice(h * HEAD_DIM, (h + 1) * HEAD_DIM)
            s = s8_ref[h]
            m_new = jnp.maximum(ms[h], jnp.max(s, axis=0, keepdims=True))
            p = jnp.exp(s - m_new)
            alpha = jnp.exp(ms[h] - m_new)
            new_ms.append(m_new)
            new_ls.append(alpha * ls[h] + jnp.sum(p, axis=0, keepdims=True))
            vc = vt_ref[0, rows_h, pl.ds(off, TK)]
            y_ref[rows_h, :] = alpha * y_ref[rows_h, :] + jnp.dot(
                vc, p.astype(BF16), preferred_element_type=F32)
        return tuple(new_ms), tuple(new_ls)

    init = (tuple(jnp.full((1, TQ), NEG, F32) for _ in range(N_HEADS)),
            tuple(jnp.zeros((1, TQ), F32) for _ in range(N_HEADS)))
    _, ls = lax.fori_loop(0, n_chunks, att_body, init)
    for h in range(N_HEADS):
        rows_h = slice(h * HEAD_DIM, (h + 1) * HEAD_DIM)
        out = y_ref[rows_h, :] / ls[h]
        ms_h = jnp.mean(out * out, axis=0, keepdims=True)
        y_ref[rows_h, :] = (out * lax.rsqrt(ms_h + EPS)) * gna_ref[rows_h, :]

    ya = (y_ref[...].T * gz_ref[0]).astype(BF16)
    yo = (jnp.dot(yconv_ref[0], wout_ref[0:D_CONV, :], preferred_element_type=F32)
          + jnp.dot(ya, wout_ref[D_CONV:D_CONV + D_ATTN, :], preferred_element_type=F32))
    gate = mod_ref[0][:, 2 * D_MODEL:3 * D_MODEL]
    o = x_ref[0] + gate * yo
    inv = lax.rsqrt(jnp.mean(o * o, axis=-1, keepdims=True) + EPS)
    o_ref[0] = (o * inv) * normf_ref[...]


def _attn_call(qt, qit, wt, k, ki, vt, gz, yconv, x, mod3, gn_attn_col, w_out_bf, norm_f):
    bsz, seq, _ = x.shape
    const = lambda b, i: (0, 0)
    rows = lambda b, i: (b, i, 0)
    cols = lambda b, i: (b, 0, i)
    whole = lambda b, i: (b, 0, 0)
    once = pl.Buffered(1)
    return pl.pallas_call(
        _attn_kernel,
        grid=(bsz, seq // TQ),
        in_specs=[
            pl.BlockSpec((1, N_HEADS * PAIR, TQ), cols),
            pl.BlockSpec((1, N_IDX_HEADS * PAIR, TQ), cols),
            pl.BlockSpec((1, N_IDX_HEADS, TQ), cols),
            pl.BlockSpec((1, seq, D_ATTN), whole, pipeline_mode=once),
            pl.BlockSpec((1, seq, PAIR), whole, pipeline_mode=once),
            pl.BlockSpec((1, D_ATTN, seq), whole, pipeline_mode=once),
            pl.BlockSpec((1, TQ, D_ATTN), rows),
            pl.BlockSpec((1, TQ, D_CONV), rows),
            pl.BlockSpec((1, TQ, D_MODEL), rows),
            pl.BlockSpec((1, 1, 3 * D_MODEL), whole),
            pl.BlockSpec((D_ATTN, 1), const),
            pl.BlockSpec((D_CONV + D_ATTN, D_MODEL), const, pipeline_mode=once),
            pl.BlockSpec((1, D_MODEL), const),
        ],
        out_specs=pl.BlockSpec((1, TQ, D_MODEL), rows),
        out_shape=jax.ShapeDtypeStruct((bsz, seq, D_MODEL), F32),
        scratch_shapes=[pltpu.VMEM((seq, TQ), F32), pltpu.VMEM((D_ATTN, TQ), F32),
                        pltpu.VMEM((N_HEADS, TK, TQ), F32), pltpu.VMEM((1, TQ), F32)],
        compiler_params=pltpu.CompilerParams(dimension_semantics=("arbitrary", "arbitrary"),
                                             vmem_limit_bytes=VMEM_LIMIT),
        name="dsa_attention",
    )(qt, qit, wt, k, ki, vt, gz, yconv, x, mod3, gn_attn_col, w_out_bf, norm_f)


def _split_in_proj(w_in):
    conv = w_in[:, 0:2048]
    aq, ak, av, az = (w_in[:, 2048 + n * 512:2560 + n * 512] for n in range(4))
    iq = w_in[:, 4096:4608]
    ik = w_in[:, 4608:4672]
    iw = w_in[:, 4672:4680]
    w_nat = jnp.concatenate([conv, az], axis=1).astype(BF16)
    pad = jnp.zeros((D_MODEL, R_END - R_W - N_IDX_HEADS), w_in.dtype)
    w_cm = jnp.concatenate([aq, ak, av, iq, ik, ik, iw, pad], axis=1).T.astype(BF16)
    return w_nat, w_cm


def kernel(x, c, positions, w_ada, b_ada, norm_in, w_in, conv_w, conv_b, gn_conv, gn_attn, w_out, norm_f):
    bsz, seq, _ = x.shape
    mod3 = _ada_call(c, w_ada, b_ada).reshape(bsz, 1, 3 * D_MODEL)
    w_nat, w_cm = _split_in_proj(w_in)
    group = jnp.arange(D_CONV) // (D_CONV // CONV_GROUPS)
    gmat = jnp.where(group[:, None] == group[None, :], 1.0 / (D_CONV // CONV_GROUPS), 0.0).astype(BF16)
    yconv, gz, qt, k, vt, qit, ki, wt = _proj_call(
        x, positions.reshape(bsz, 1, seq), mod3, norm_in.reshape(1, D_MODEL), w_nat, w_cm,
        conv_w, conv_b.reshape(1, D_CONV), gn_conv.reshape(1, D_CONV), gmat)
    return _attn_call(qt, qit, wt, k, ki, vt, gz, yconv, x, mod3, gn_attn.reshape(D_ATTN, 1),
                      w_out.astype(BF16), norm_f.reshape(1, D_MODEL))
```

```python
import functools
import math

import jax
import jax.numpy as jnp
from jax import lax
from jax.experimental import pallas as pl
from jax.experimental.pallas import tpu as pltpu

D_MODEL = 1024
D_CONV = 512
D_ATTN = 512
HEAD_DIM = 64
N_HEADS = 8
CONV_GROUPS = 8
CONV_WIDTH = 3
ROPE_HALF = 8
ROPE_THETA = 500000.0
N_IDX_HEADS = 8
IDX_DIM = 64
TOPK = 256
EPS = 1e-6

LANES = 128
PAIR = 2 * HEAD_DIM
TS = 512
TQ = 512
TK = 512
HEADS_PER_PASS = 4
NEG = -1e30
LOG2_E = math.log2(math.e)
F32_MAX = float(jnp.finfo(jnp.float32).max)
INT_MIN = -(2 ** 31)
KEY_LOWEST_FINITE = INT_MIN + 0x00800000
VMEM_LIMIT = 56 * 1024 * 1024

F32 = jnp.float32
BF16 = jnp.bfloat16
I32 = jnp.int32


def _silu(v):
    return v * jax.nn.sigmoid(v)


def _ada_kernel(c_ref, w_ref, b_ref, o_ref):
    o_ref[...] = jnp.dot(_silu(c_ref[...]), w_ref[...], precision=lax.Precision.HIGHEST,
                         preferred_element_type=F32) + b_ref[...]


def _ada_call(c, w_ada, b_ada):
    bsz = c.shape[0]
    n_blk = 3
    return pl.pallas_call(
        _ada_kernel,
        grid=(n_blk,),
        in_specs=[pl.BlockSpec((bsz, D_MODEL), lambda n: (0, 0)),
                  pl.BlockSpec((D_MODEL, D_MODEL), lambda n: (0, n)),
                  pl.BlockSpec((1, D_MODEL), lambda n: (0, n))],
        out_specs=pl.BlockSpec((bsz, D_MODEL), lambda n: (0, n)),
        out_shape=jax.ShapeDtypeStruct((bsz, 3 * D_MODEL), F32),
        compiler_params=pltpu.CompilerParams(dimension_semantics=("arbitrary",)),
        name="adaln_mod",
    )(c, w_ada, b_ada.reshape(1, 3 * D_MODEL))


R_Q, R_K, R_V, R_QI, R_KI, R_W, R_END = 0, 512, 1024, 1536, 2048, 2176, 2192


def _rope_cm(xt, n_heads, cs, sn):
    t = xt.shape[-1]
    xr = xt.reshape(n_heads, HEAD_DIM, t)
    x1 = xr[:, 0:ROPE_HALF, :]
    x2 = xr[:, ROPE_HALF:2 * ROPE_HALF, :]
    r1 = x1 * cs - x2 * sn
    r2 = x1 * sn + x2 * cs
    return jnp.concatenate([r1, r2, xr[:, 2 * ROPE_HALF:, :]], axis=1).reshape(n_heads * HEAD_DIM, t)


def _pad_heads_to_pairs(xt):
    z = jnp.zeros((HEAD_DIM, xt.shape[-1]), xt.dtype)
    pieces = []
    for h in range(N_HEADS):
        blk = xt[h * HEAD_DIM:(h + 1) * HEAD_DIM, :]
        pieces += [blk, z] if h % 2 == 0 else [z, blk]
    return jnp.concatenate(pieces, axis=0)


def _proj_kernel(x_ref, pos_ref, mod_ref, normin_ref, wnat_ref, wcm_ref, convw_ref, convb_ref,
                 gnc_ref, gmat_ref,
                 yconv_ref, gz_ref, qt_ref, k_ref, vt_ref, qit_ref, ki_ref, wt_ref,
                 sbuf_ref):
    j = pl.program_id(1)
    x = x_ref[0]
    inv = lax.rsqrt(jnp.mean(x * x, axis=-1, keepdims=True) + EPS)
    mod = mod_ref[0]
    shift = mod[:, 0:D_MODEL]
    scale = mod[:, D_MODEL:2 * D_MODEL]
    h = (x * inv) * normin_ref[...] * (1.0 + scale) + shift
    hb = h.astype(BF16)

    nat = jnp.dot(hb, wnat_ref[...], preferred_element_type=F32)
    cb = nat[:, 0:512]
    cc = nat[:, 512:1024]
    cu = nat[:, 1024:1536]
    cz = nat[:, 1536:2048]
    az = nat[:, 2048:2560]
    gz_ref[0] = _silu(az)

    s = cc * cu

    @pl.when(j == 0)
    def _():
        sbuf_ref[0:8, :] = jnp.zeros((8, D_CONV), F32)

    sbuf_ref[8:TS + 8, :] = s
    s1 = sbuf_ref[7:TS + 7, :]
    s2 = sbuf_ref[6:TS + 6, :]
    cw = convw_ref[...]
    conv = convb_ref[...] + ((cw[0:1, :] * s2 + cw[1:2, :] * s1) + cw[2:3, :] * s)
    sbuf_ref[0:8, :] = sbuf_ref[TS:TS + 8, :]
    y = cb * conv
    ysq = y * y
    hi = ysq.astype(BF16)
    lo = (ysq - hi.astype(F32)).astype(BF16)
    gm = gmat_ref[...]
    ms = jnp.dot(hi, gm, preferred_element_type=F32) + jnp.dot(lo, gm, preferred_element_type=F32)
    yconv_ref[0] = ((y * lax.rsqrt(ms + EPS)) * gnc_ref[...] * _silu(cz)).astype(BF16)

    ht = h.T.astype(BF16)

    def proj_cm(r0, r1):
        return jnp.dot(wcm_ref[r0:r1, :], ht, preferred_element_type=F32)

    pos = pos_ref[0].astype(F32)
    jj = lax.broadcasted_iota(I32, (ROPE_HALF, 1), 0).astype(F32)
    theta = jnp.full((ROPE_HALF, 1), ROPE_THETA, F32)
    inv_freq = jnp.exp(jnp.log(theta) * (jj * (-1.0 / ROPE_HALF)))
    ang = inv_freq * pos
    cs = jnp.cos(ang)
    sn = jnp.sin(ang)

    scale_pow2 = HEAD_DIM ** -0.5
    q = _rope_cm(proj_cm(R_Q, R_K), N_HEADS, cs, sn) * (scale_pow2 * LOG2_E)
    qt_ref[0] = _pad_heads_to_pairs(q).astype(BF16)
    k = _rope_cm(proj_cm(R_K, R_V), N_HEADS, cs, sn)
    k_ref[0] = k.T.astype(BF16)
    vt_ref[0] = proj_cm(R_V, R_QI).astype(BF16)
    qi = _rope_cm(proj_cm(R_QI, R_KI), N_IDX_HEADS, cs, sn) * scale_pow2
    qit_ref[0] = _pad_heads_to_pairs(qi).astype(BF16)
    kiw = proj_cm(R_KI, R_END)
    ki2 = _rope_cm(kiw[0:PAIR, :], 2, cs, sn)
    ki_ref[0] = ki2.T.astype(BF16)
    wt_ref[0] = kiw[PAIR:PAIR + N_IDX_HEADS, :] * (N_IDX_HEADS ** -0.5)


def _proj_call(x, pos3, mod3, norm_in, w_nat, w_cm, conv_w, conv_b, gn_conv, gmat):
    bsz, seq, _ = x.shape
    const = lambda b, j: (0, 0)
    rows = lambda b, j: (b, j, 0)
    cols = lambda b, j: (b, 0, j)
    out_shapes = (
        jax.ShapeDtypeStruct((bsz, seq, D_CONV), BF16),
        jax.ShapeDtypeStruct((bsz, seq, D_ATTN), F32),
        jax.ShapeDtypeStruct((bsz, N_HEADS * PAIR, seq), BF16),
        jax.ShapeDtypeStruct((bsz, seq, D_ATTN), BF16),
        jax.ShapeDtypeStruct((bsz, D_ATTN, seq), BF16),
        jax.ShapeDtypeStruct((bsz, N_IDX_HEADS * PAIR, seq), BF16),
        jax.ShapeDtypeStruct((bsz, seq, PAIR), BF16),
        jax.ShapeDtypeStruct((bsz, N_IDX_HEADS, seq), F32),
    )
    return pl.pallas_call(
        _proj_kernel,
        grid=(bsz, seq // TS),
        in_specs=[
            pl.BlockSpec((1, TS, D_MODEL), rows),
            pl.BlockSpec((1, 1, TS), cols),
            pl.BlockSpec((1, 1, 3 * D_MODEL), lambda b, j: (b, 0, 0)),
            pl.BlockSpec((1, D_MODEL), const),
            pl.BlockSpec(w_nat.shape, const),
            pl.BlockSpec(w_cm.shape, const),
            pl.BlockSpec((CONV_WIDTH, D_CONV), const),
            pl.BlockSpec((1, D_CONV), const),
            pl.BlockSpec((1, D_CONV), const),
            pl.BlockSpec((D_CONV, D_CONV), const),
        ],
        out_specs=(
            pl.BlockSpec((1, TS, D_CONV), rows),
            pl.BlockSpec((1, TS, D_ATTN), rows),
            pl.BlockSpec((1, N_HEADS * PAIR, TS), cols),
            pl.BlockSpec((1, TS, D_ATTN), rows),
            pl.BlockSpec((1, D_ATTN, TS), cols),
            pl.BlockSpec((1, N_IDX_HEADS * PAIR, TS), cols),
            pl.BlockSpec((1, TS, PAIR), rows),
            pl.BlockSpec((1, N_IDX_HEADS, TS), cols),
        ),
        out_shape=out_shapes,
        scratch_shapes=[pltpu.VMEM((TS + 8, D_CONV), F32)],
        compiler_params=pltpu.CompilerParams(dimension_semantics=("arbitrary", "arbitrary"),
                                             vmem_limit_bytes=VMEM_LIMIT),
        name="in_proj",
    )(x, pos3, mod3, norm_in, w_nat, w_cm, conv_w, conv_b, gn_conv, gmat)


def _key_to_f32(key):
    bits = jnp.where(key < 0, key ^ 0x7FFFFFFF, key)
    return lax.bitcast_convert_type(bits, F32)


def _attn_kernel(qt_ref, qit_ref, wt_ref, k_ref, ki_ref, vt_ref, gz_ref, yconv_ref, x_ref, mod_ref,
                 gna_ref, wout_ref, normf_ref, o_ref, sc_ref, y_ref, s8_ref, thr_ref):
    i = pl.program_id(1)
    n_chunks = i + 1
    wt = wt_ref[0]
    row = lax.broadcasted_iota(I32, (TK, TQ), 0)
    col = lax.broadcasted_iota(I32, (TK, TQ), 1)

    def chunk_off(c):
        return pl.multiple_of(c * TK, TK)

    def score_chunk(c):
        kc = ki_ref[0, pl.ds(chunk_off(c), TK), :]
        acc = jnp.zeros((TK, TQ), F32)
        for hh in range(N_IDX_HEADS):
            d = jnp.dot(kc, qit_ref[0, hh * PAIR:(hh + 1) * PAIR, :], preferred_element_type=F32)
            acc = acc + wt[hh:hh + 1, :] * jnp.maximum(d, 0.0)
        return acc

    def score_body(c, carry):
        sc_ref[pl.ds(chunk_off(c), TK), :] = score_chunk(c)
        return carry

    lax.fori_loop(0, i, score_body, 0)
    sc_ref[pl.ds(chunk_off(i), TK), :] = jnp.where(row <= col, score_chunk(i), -jnp.inf)

    def count_chunks(pred):
        def body(c, acc):
            tile = sc_ref[pl.ds(chunk_off(c), TK), :]
            return acc + jnp.sum(pred(tile, c).reshape(TK // 8, 8, TQ), axis=0)
        acc = lax.fori_loop(0, n_chunks, body, jnp.zeros((8, TQ), I32))
        return jnp.sum(acc, axis=0, keepdims=True)

    def count_ge(thr):
        return count_chunks(lambda tile, c: (tile >= thr).astype(I32))

    def bit_body(b, carry):
        lo_u, cnt_lo = carry
        cand_u = lo_u | lax.shift_left(jnp.int32(1), 31 - b)
        cnt = count_ge(_key_to_f32(cand_u ^ INT_MIN))
        ok = cnt >= TOPK
        return jnp.where(ok, cand_u, lo_u), jnp.where(ok, cnt, cnt_lo)

    lo_u, cnt_lo = lax.fori_loop(0, 32, bit_body, (jnp.zeros((1, TQ), I32), jnp.zeros((1, TQ), I32)))
    lo = lo_u ^ INT_MIN
    has_kth = lo >= KEY_LOWEST_FINITE
    thr0 = jnp.where(has_kth, _key_to_f32(lo), -F32_MAX)
    tie0 = jnp.where(has_kth, cnt_lo, 0) > TOPK

    thr_ref[...] = thr0

    @pl.when(jnp.max(tie0.astype(I32)) > 0)
    def _():
        def midpoint(lo_f, hi_f):
            return lo_f + (hi_f - lo_f) * 0.5

        def splits(lo_f, hi_f):
            mid = midpoint(lo_f, hi_f)
            return jnp.where(mid > lo_f, jnp.where(mid < hi_f, 1, 0), 0).astype(I32)

        def refine_cond(carry):
            lo_f, hi_f, _, it = carry
            return jnp.logical_and(jnp.max(splits(lo_f, hi_f)) > 0, it < 64)

        def refine_body(carry):
            lo_f, hi_f, cnt_f, it = carry
            mid = midpoint(lo_f, hi_f)
            cnt = count_ge(mid)
            up = jnp.where(splits(lo_f, hi_f) > 0, jnp.where(cnt >= TOPK, 1, 0), 0) > 0
            down = jnp.where(splits(lo_f, hi_f) > 0, jnp.where(cnt >= TOPK, 0, 1), 0) > 0
            return (jnp.where(up, mid, lo_f), jnp.where(down, mid, hi_f),
                    jnp.where(up, cnt, cnt_f), it + 1)

        above0 = jnp.where(tie0, _key_to_f32(lo + 1), thr0)
        thr, _, cnt_thr, _ = lax.while_loop(refine_cond, refine_body,
                                            (thr0, above0, cnt_lo, jnp.int32(0)))
        thr_ref[...] = thr

        tie = jnp.where(tie0, cnt_thr, 0) > TOPK
        need = TOPK - count_chunks(lambda tile, c: (tile > thr).astype(I32))

        def ties_before(p):
            def pred(tile, c):
                rid = row + c * TK
                return jnp.where(tile == thr, jnp.where(rid < p, 1, 0), 0).astype(I32)
            return count_chunks(pred)

        def idx_body(b, p):
            cand = p | lax.shift_left(jnp.int32(1), 11 - b)
            return jnp.where(ties_before(cand) < need, cand, p)

        p_last = lax.fori_loop(0, 12, idx_body, jnp.zeros((1, TQ), I32))
        p_last = jnp.where(tie, p_last, jnp.int32(2 ** 30))

        def drop_body(c, carry):
            tile = sc_ref[pl.ds(chunk_off(c), TK), :]
            rid = row + c * TK
            dropped = jnp.where(rid > p_last, -jnp.inf, tile)
            sc_ref[pl.ds(chunk_off(c), TK), :] = jnp.where(tile == thr, dropped, tile)
            return carry

        lax.fori_loop(0, n_chunks, drop_body, 0)

    thr = thr_ref[...]

    def bias_body(c, carry):
        tile = sc_ref[pl.ds(chunk_off(c), TK), :]
        sc_ref[pl.ds(chunk_off(c), TK), :] = jnp.where(tile >= thr, 0.0, NEG)
        return carry

    lax.fori_loop(0, n_chunks, bias_body, 0)

    y_ref[...] = jnp.zeros((D_ATTN, TQ), F32)

    def att_body(c, carry, heads):
        ms, ls = carry
        off = chunk_off(c)
        new_ms, new_ls = [], []
        bias = sc_ref[pl.ds(off, TK), :]
        for n, h in enumerate(heads):
            lane0 = (h // 2) * PAIR
            q2 = qt_ref[0, h * PAIR:(h + 1) * PAIR, :]
            k2 = k_ref[0, pl.ds(off, TK), lane0:lane0 + PAIR]
            s8_ref[n] = jnp.dot(k2, q2, preferred_element_type=F32) + bias
        for n, h in enumerate(heads):
            rows_h = slice(h * HEAD_DIM, (h + 1) * HEAD_DIM)
            s = s8_ref[n]
            m_new = jnp.maximum(ms[n], jnp.max(s, axis=0, keepdims=True))
            p = jnp.exp2(s - m_new)
            alpha = jnp.exp2(ms[n] - m_new)
            new_ms.append(m_new)
            new_ls.append(alpha * ls[n] + jnp.sum(p, axis=0, keepdims=True))
            vc = vt_ref[0, rows_h, pl.ds(off, TK)]
            y_ref[rows_h, :] = alpha * y_ref[rows_h, :] + jnp.dot(
                vc, p.astype(BF16), preferred_element_type=F32)
        return tuple(new_ms), tuple(new_ls)

    ls = []
    for g in range(N_HEADS // HEADS_PER_PASS):
        heads = tuple(range(g * HEADS_PER_PASS, (g + 1) * HEADS_PER_PASS))
        init = (tuple(jnp.full((1, TQ), NEG, F32) for _ in heads),
                tuple(jnp.zeros((1, TQ), F32) for _ in heads))
        _, ls_g = lax.fori_loop(0, n_chunks, functools.partial(att_body, heads=heads), init)
        ls += list(ls_g)
    for h in range(N_HEADS):
        rows_h = slice(h * HEAD_DIM, (h + 1) * HEAD_DIM)
        out = y_ref[rows_h, :] / ls[h]
        ms_h = jnp.mean(out * out, axis=0, keepdims=True)
        y_ref[rows_h, :] = (out * lax.rsqrt(ms_h + EPS)) * gna_ref[rows_h, :]

    ya = (y_ref[...].T * gz_ref[0]).astype(BF16)
    yo = (jnp.dot(yconv_ref[0], wout_ref[0:D_CONV, :], preferred_element_type=F32)
          + jnp.dot(ya, wout_ref[D_CONV:D_CONV + D_ATTN, :], preferred_element_type=F32))
    gate = mod_ref[0][:, 2 * D_MODEL:3 * D_MODEL]
    o = x_ref[0] + gate * yo
    inv = lax.rsqrt(jnp.mean(o * o, axis=-1, keepdims=True) + EPS)
    o_ref[0] = (o * inv) * normf_ref[...]


def _attn_call(qt, qit, wt, k, ki, vt, gz, yconv, x, mod3, gn_attn_col, w_out_bf, norm_f):
    bsz, seq, _ = x.shape
    const = lambda b, i: (0, 0)
    rows = lambda b, i: (b, i, 0)
    cols = lambda b, i: (b, 0, i)
    whole = lambda b, i: (b, 0, 0)
    once = pl.Buffered(1)
    return pl.pallas_call(
        _attn_kernel,
        grid=(bsz, seq // TQ),
        in_specs=[
            pl.BlockSpec((1, N_HEADS * PAIR, TQ), cols),
            pl.BlockSpec((1, N_IDX_HEADS * PAIR, TQ), cols),
            pl.BlockSpec((1, N_IDX_HEADS, TQ), cols),
            pl.BlockSpec((1, seq, D_ATTN), whole, pipeline_mode=once),
            pl.BlockSpec((1, seq, PAIR), whole, pipeline_mode=once),
            pl.BlockSpec((1, D_ATTN, seq), whole, pipeline_mode=once),
            pl.BlockSpec((1, TQ, D_ATTN), rows),
            pl.BlockSpec((1, TQ, D_CONV), rows),
            pl.BlockSpec((1, TQ, D_MODEL), rows),
            pl.BlockSpec((1, 1, 3 * D_MODEL), whole),
            pl.BlockSpec((D_ATTN, 1), const),
            pl.BlockSpec((D_CONV + D_ATTN, D_MODEL), const, pipeline_mode=once),
            pl.BlockSpec((1, D_MODEL), const),
        ],
        out_specs=pl.BlockSpec((1, TQ, D_MODEL), rows),
        out_shape=jax.ShapeDtypeStruct((bsz, seq, D_MODEL), F32),
        scratch_shapes=[pltpu.VMEM((seq, TQ), F32), pltpu.VMEM((D_ATTN, TQ), F32),
                        pltpu.VMEM((HEADS_PER_PASS, TK, TQ), F32), pltpu.VMEM((1, TQ), F32)],
        compiler_params=pltpu.CompilerParams(dimension_semantics=("arbitrary", "arbitrary"),
                                             vmem_limit_bytes=VMEM_LIMIT),
        name="dsa_attention",
    )(qt, qit, wt, k, ki, vt, gz, yconv, x, mod3, gn_attn_col, w_out_bf, norm_f)


def _split_in_proj(w_in):
    conv = w_in[:, 0:2048]
    aq, ak, av, az = (w_in[:, 2048 + n * 512:2560 + n * 512] for n in range(4))
    iq = w_in[:, 4096:4608]
    ik = w_in[:, 4608:4672]
    iw = w_in[:, 4672:4680]
    w_nat = jnp.concatenate([conv, az], axis=1).astype(BF16)
    pad = jnp.zeros((D_MODEL, R_END - R_W - N_IDX_HEADS), w_in.dtype)
    w_cm = jnp.concatenate([aq, ak, av, iq, ik, ik, iw, pad], axis=1).T.astype(BF16)
    return w_nat, w_cm


def kernel(x, c, positions, w_ada, b_ada, norm_in, w_in, conv_w, conv_b, gn_conv, gn_attn, w_out, norm_f):
    bsz, seq, _ = x.shape
    mod3 = _ada_call(c, w_ada, b_ada).reshape(bsz, 1, 3 * D_MODEL)
    w_nat, w_cm = _split_in_proj(w_in)
    group = jnp.arange(D_CONV) // (D_CONV // CONV_GROUPS)
    gmat = jnp.where(group[:, None] == group[None, :], 1.0 / (D_CONV // CONV_GROUPS), 0.0).astype(BF16)
    yconv, gz, qt, k, vt, qit, ki, wt = _proj_call(
        x, positions.reshape(bsz, 1, seq), mod3, norm_in.reshape(1, D_MODEL), w_nat, w_cm,
        conv_w, conv_b.reshape(1, D_CONV), gn_conv.reshape(1, D_CONV), gmat)
    return _attn_call(qt, qit, wt, k, ki, vt, gz, yconv, x, mod3, gn_attn.reshape(D_ATTN, 1),
                      w_out.astype(BF16), norm_f.reshape(1, D_MODEL))
```

```python
import functools
import math

import jax
import jax.numpy as jnp
from jax import lax
from jax.experimental import pallas as pl
from jax.experimental.pallas import tpu as pltpu

D_MODEL = 1024
D_CONV = 512
D_ATTN = 512
HEAD_DIM = 64
N_HEADS = 8
CONV_GROUPS = 8
CONV_WIDTH = 3
ROPE_HALF = 8
ROPE_THETA = 500000.0
N_IDX_HEADS = 8
IDX_DIM = 64
TOPK = 256
EPS = 1e-6

LANES = 128
PAIR = 2 * HEAD_DIM
TS = 512
TQ = 512
TK = 512
HEADS_PER_PASS = 4
FINE_BITS = 18
NEG = -1e30
LOG2_E = math.log2(math.e)
F32_MAX = float(jnp.finfo(jnp.float32).max)
INT_MIN = -(2 ** 31)
KEY_LOWEST_FINITE = INT_MIN + 0x00800000
KEY_NEG_INF_BF16 = INT_MIN + 0x007F0000
VMEM_LIMIT = 56 * 1024 * 1024

F32 = jnp.float32
BF16 = jnp.bfloat16
I32 = jnp.int32


def _silu(v):
    return v * jax.nn.sigmoid(v)


def _ada_kernel(c_ref, w_ref, b_ref, o_ref):
    o_ref[...] = jnp.dot(_silu(c_ref[...]), w_ref[...], precision=lax.Precision.HIGHEST,
                         preferred_element_type=F32) + b_ref[...]


def _ada_call(c, w_ada, b_ada):
    bsz = c.shape[0]
    n_blk = 3
    return pl.pallas_call(
        _ada_kernel,
        grid=(n_blk,),
        in_specs=[pl.BlockSpec((bsz, D_MODEL), lambda n: (0, 0)),
                  pl.BlockSpec((D_MODEL, D_MODEL), lambda n: (0, n)),
                  pl.BlockSpec((1, D_MODEL), lambda n: (0, n))],
        out_specs=pl.BlockSpec((bsz, D_MODEL), lambda n: (0, n)),
        out_shape=jax.ShapeDtypeStruct((bsz, 3 * D_MODEL), F32),
        compiler_params=pltpu.CompilerParams(dimension_semantics=("arbitrary",)),
        name="adaln_mod",
    )(c, w_ada, b_ada.reshape(1, 3 * D_MODEL))


R_Q, R_K, R_V, R_QI, R_KI, R_W, R_END = 0, 512, 1024, 1536, 2048, 2176, 2192


def _rope_cm(xt, n_heads, cs, sn):
    t = xt.shape[-1]
    xr = xt.reshape(n_heads, HEAD_DIM, t)
    x1 = xr[:, 0:ROPE_HALF, :]
    x2 = xr[:, ROPE_HALF:2 * ROPE_HALF, :]
    r1 = x1 * cs - x2 * sn
    r2 = x1 * sn + x2 * cs
    return jnp.concatenate([r1, r2, xr[:, 2 * ROPE_HALF:, :]], axis=1).reshape(n_heads * HEAD_DIM, t)


def _pad_heads_to_pairs(xt):
    z = jnp.zeros((HEAD_DIM, xt.shape[-1]), xt.dtype)
    pieces = []
    for h in range(N_HEADS):
        blk = xt[h * HEAD_DIM:(h + 1) * HEAD_DIM, :]
        pieces += [blk, z] if h % 2 == 0 else [z, blk]
    return jnp.concatenate(pieces, axis=0)


def _proj_kernel(x_ref, pos_ref, mod_ref, normin_ref, wnat_ref, wcm_ref, convw_ref, convb_ref,
                 gnc_ref, gmat_ref,
                 yconv_ref, gz_ref, qt_ref, k_ref, vt_ref, qit_ref, ki_ref, wt_ref,
                 sbuf_ref):
    j = pl.program_id(1)
    x = x_ref[0]
    inv = lax.rsqrt(jnp.mean(x * x, axis=-1, keepdims=True) + EPS)
    mod = mod_ref[0]
    shift = mod[:, 0:D_MODEL]
    scale = mod[:, D_MODEL:2 * D_MODEL]
    h = (x * inv) * normin_ref[...] * (1.0 + scale) + shift
    hb = h.astype(BF16)

    nat = jnp.dot(hb, wnat_ref[...], preferred_element_type=F32)
    cb = nat[:, 0:512]
    cc = nat[:, 512:1024]
    cu = nat[:, 1024:1536]
    cz = nat[:, 1536:2048]
    az = nat[:, 2048:2560]
    gz_ref[0] = _silu(az)

    s = cc * cu

    @pl.when(j == 0)
    def _():
        sbuf_ref[0:8, :] = jnp.zeros((8, D_CONV), F32)

    sbuf_ref[8:TS + 8, :] = s
    s1 = sbuf_ref[7:TS + 7, :]
    s2 = sbuf_ref[6:TS + 6, :]
    cw = convw_ref[...]
    conv = convb_ref[...] + ((cw[0:1, :] * s2 + cw[1:2, :] * s1) + cw[2:3, :] * s)
    sbuf_ref[0:8, :] = sbuf_ref[TS:TS + 8, :]
    y = cb * conv
    ysq = y * y
    hi = ysq.astype(BF16)
    lo = (ysq - hi.astype(F32)).astype(BF16)
    gm = gmat_ref[...]
    ms = jnp.dot(hi, gm, preferred_element_type=F32) + jnp.dot(lo, gm, preferred_element_type=F32)
    yconv_ref[0] = ((y * lax.rsqrt(ms + EPS)) * gnc_ref[...] * _silu(cz)).astype(BF16)

    ht = h.T.astype(BF16)

    def proj_cm(r0, r1):
        return jnp.dot(wcm_ref[r0:r1, :], ht, preferred_element_type=F32)

    pos = pos_ref[0].astype(F32)
    jj = lax.broadcasted_iota(I32, (ROPE_HALF, 1), 0).astype(F32)
    theta = jnp.full((ROPE_HALF, 1), ROPE_THETA, F32)
    inv_freq = jnp.exp(jnp.log(theta) * (jj * (-1.0 / ROPE_HALF)))
    ang = inv_freq * pos
    cs = jnp.cos(ang)
    sn = jnp.sin(ang)

    scale_pow2 = HEAD_DIM ** -0.5
    q = _rope_cm(proj_cm(R_Q, R_K), N_HEADS, cs, sn) * (scale_pow2 * LOG2_E)
    qt_ref[0] = _pad_heads_to_pairs(q).astype(BF16)
    k = _rope_cm(proj_cm(R_K, R_V), N_HEADS, cs, sn)
    k_ref[0] = k.T.astype(BF16)
    vt_ref[0] = proj_cm(R_V, R_QI).astype(BF16)
    qi = _rope_cm(proj_cm(R_QI, R_KI), N_IDX_HEADS, cs, sn) * scale_pow2
    qit_ref[0] = _pad_heads_to_pairs(qi).astype(BF16)
    kiw = proj_cm(R_KI, R_END)
    ki2 = _rope_cm(kiw[0:PAIR, :], 2, cs, sn)
    ki_ref[0] = ki2.T.astype(BF16)
    wt_ref[0] = kiw[PAIR:PAIR + N_IDX_HEADS, :] * (N_IDX_HEADS ** -0.5)


def _proj_call(x, pos3, mod3, norm_in, w_nat, w_cm, conv_w, conv_b, gn_conv, gmat):
    bsz, seq, _ = x.shape
    const = lambda b, j: (0, 0)
    rows = lambda b, j: (b, j, 0)
    cols = lambda b, j: (b, 0, j)
    out_shapes = (
        jax.ShapeDtypeStruct((bsz, seq, D_CONV), BF16),
        jax.ShapeDtypeStruct((bsz, seq, D_ATTN), F32),
        jax.ShapeDtypeStruct((bsz, N_HEADS * PAIR, seq), BF16),
        jax.ShapeDtypeStruct((bsz, seq, D_ATTN), BF16),
        jax.ShapeDtypeStruct((bsz, D_ATTN, seq), BF16),
        jax.ShapeDtypeStruct((bsz, N_IDX_HEADS * PAIR, seq), BF16),
        jax.ShapeDtypeStruct((bsz, seq, PAIR), BF16),
        jax.ShapeDtypeStruct((bsz, N_IDX_HEADS, seq), F32),
    )
    return pl.pallas_call(
        _proj_kernel,
        grid=(bsz, seq // TS),
        in_specs=[
            pl.BlockSpec((1, TS, D_MODEL), rows),
            pl.BlockSpec((1, 1, TS), cols),
            pl.BlockSpec((1, 1, 3 * D_MODEL), lambda b, j: (b, 0, 0)),
            pl.BlockSpec((1, D_MODEL), const),
            pl.BlockSpec(w_nat.shape, const),
            pl.BlockSpec(w_cm.shape, const),
            pl.BlockSpec((CONV_WIDTH, D_CONV), const),
            pl.BlockSpec((1, D_CONV), const),
            pl.BlockSpec((1, D_CONV), const),
            pl.BlockSpec((D_CONV, D_CONV), const),
        ],
        out_specs=(
            pl.BlockSpec((1, TS, D_CONV), rows),
            pl.BlockSpec((1, TS, D_ATTN), rows),
            pl.BlockSpec((1, N_HEADS * PAIR, TS), cols),
            pl.BlockSpec((1, TS, D_ATTN), rows),
            pl.BlockSpec((1, D_ATTN, TS), cols),
            pl.BlockSpec((1, N_IDX_HEADS * PAIR, TS), cols),
            pl.BlockSpec((1, TS, PAIR), rows),
            pl.BlockSpec((1, N_IDX_HEADS, TS), cols),
        ),
        out_shape=out_shapes,
        scratch_shapes=[pltpu.VMEM((TS + 8, D_CONV), F32)],
        compiler_params=pltpu.CompilerParams(dimension_semantics=("arbitrary", "arbitrary"),
                                             vmem_limit_bytes=VMEM_LIMIT),
        name="in_proj",
    )(x, pos3, mod3, norm_in, w_nat, w_cm, conv_w, conv_b, gn_conv, gmat)


def _key_to_f32(key):
    bits = jnp.where(key < 0, key ^ 0x7FFFFFFF, key)
    return lax.bitcast_convert_type(bits, F32)


def _attn_kernel(qt_ref, qit_ref, wt_ref, k_ref, ki_ref, vt_ref, gz_ref, yconv_ref, x_ref, mod_ref,
                 gna_ref, wout_ref, normf_ref, o_ref, sc_ref, sb_ref, y_ref, s8_ref, thr_ref):
    i = pl.program_id(1)
    n_chunks = i + 1
    wt = wt_ref[0]
    row = lax.broadcasted_iota(I32, (TK, TQ), 0)
    col = lax.broadcasted_iota(I32, (TK, TQ), 1)

    def chunk_off(c):
        return pl.multiple_of(c * TK, TK)

    def score_chunk(c):
        kc = ki_ref[0, pl.ds(chunk_off(c), TK), :]
        acc = jnp.zeros((TK, TQ), F32)
        for hh in range(N_IDX_HEADS):
            d = jnp.dot(kc, qit_ref[0, hh * PAIR:(hh + 1) * PAIR, :], preferred_element_type=F32)
            acc = acc + wt[hh:hh + 1, :] * jnp.maximum(d, 0.0)
        return acc

    def put_scores(c, tile):
        sc_ref[pl.ds(chunk_off(c), TK), :] = tile
        sb_ref[pl.ds(chunk_off(c), TK), :] = tile.astype(BF16)

    def score_body(c, carry):
        put_scores(c, score_chunk(c))
        return carry

    lax.fori_loop(0, i, score_body, 0)
    put_scores(i, jnp.where(row <= col, score_chunk(i), -jnp.inf))

    def count_chunks(pred):
        def body(c, acc):
            tile = sc_ref[pl.ds(chunk_off(c), TK), :]
            return acc + jnp.sum(pred(tile, c).reshape(TK // 8, 8, TQ), axis=0)
        acc = lax.fori_loop(0, n_chunks, body, jnp.zeros((8, TQ), I32))
        return jnp.sum(acc, axis=0, keepdims=True)

    def count_ge(thr):
        return count_chunks(lambda tile, c: (tile >= thr).astype(I32))

    def count_ge_bf16(thr_b):
        def body(c, acc):
            tile = sb_ref[pl.ds(chunk_off(c), TK), :]
            hit = jnp.where(tile >= thr_b, jnp.ones((), BF16), jnp.zeros((), BF16))
            hit = hit.reshape(TK // 16, 16, TQ)
            part = hit[0]
            for r in range(1, TK // 16):
                part = part + hit[r]
            return acc + part.astype(F32)
        acc = lax.fori_loop(0, n_chunks, body, jnp.zeros((16, TQ), F32))
        return jnp.sum(acc, axis=0, keepdims=True).astype(I32)

    def coarse_body(b, lo_u):
        cand_u = lo_u | lax.shift_left(jnp.int32(1), 31 - b)
        key = cand_u ^ INT_MIN
        bits = jnp.where(key < 0, key ^ 0x7FFF0000, key)
        cand = lax.bitcast_convert_type(bits, F32).astype(BF16)
        cnt = jnp.where(key < KEY_NEG_INF_BF16, TOPK, count_ge_bf16(cand))
        return jnp.where(cnt >= TOPK, cand_u, lo_u)

    coarse_u = lax.fori_loop(0, 16, coarse_body, jnp.zeros((1, TQ), I32))
    coarse_key = coarse_u ^ INT_MIN
    coarse_key = jnp.where(coarse_key < 0, coarse_key | 0xFFFF, coarse_key)
    base_key = coarse_key - (1 << (FINE_BITS - 1))

    def fine_body(b, carry):
        off, cnt_lo = carry
        cand_off = off | jnp.where(b == 0, 0, lax.shift_left(jnp.int32(1), FINE_BITS - b))
        cnt = count_ge(_key_to_f32(base_key + cand_off))
        ok = jnp.where(b == 0, TOPK, cnt) >= TOPK
        return jnp.where(ok, cand_off, off), jnp.where(ok, cnt, cnt_lo)

    off, cnt_lo = lax.fori_loop(0, FINE_BITS + 1, fine_body,
                                (jnp.zeros((1, TQ), I32), jnp.zeros((1, TQ), I32)))
    lo = base_key + off
    has_kth = lo >= KEY_LOWEST_FINITE
    thr0 = jnp.where(has_kth, _key_to_f32(lo), -F32_MAX)
    tie0 = jnp.where(has_kth, cnt_lo, 0) > TOPK

    thr_ref[...] = thr0

    @pl.when(jnp.max(tie0.astype(I32)) > 0)
    def _():
        def midpoint(lo_f, hi_f):
            return lo_f + (hi_f - lo_f) * 0.5

        def splits(lo_f, hi_f):
            mid = midpoint(lo_f, hi_f)
            return jnp.where(mid > lo_f, jnp.where(mid < hi_f, 1, 0), 0).astype(I32)

        def refine_cond(carry):
            lo_f, hi_f, _, it = carry
            return jnp.logical_and(jnp.max(splits(lo_f, hi_f)) > 0, it < 64)

        def refine_body(carry):
            lo_f, hi_f, cnt_f, it = carry
            mid = midpoint(lo_f, hi_f)
            cnt = count_ge(mid)
            up = jnp.where(splits(lo_f, hi_f) > 0, jnp.where(cnt >= TOPK, 1, 0), 0) > 0
            down = jnp.where(splits(lo_f, hi_f) > 0, jnp.where(cnt >= TOPK, 0, 1), 0) > 0
            return (jnp.where(up, mid, lo_f), jnp.where(down, mid, hi_f),
                    jnp.where(up, cnt, cnt_f), it + 1)

        above0 = jnp.where(tie0, _key_to_f32(lo + 1), thr0)
        thr, _, cnt_thr, _ = lax.while_loop(refine_cond, refine_body,
                                            (thr0, above0, cnt_lo, jnp.int32(0)))
        thr_ref[...] = thr

        tie = jnp.where(tie0, cnt_thr, 0) > TOPK
        need = TOPK - count_chunks(lambda tile, c: (tile > thr).astype(I32))

        def ties_before(p):
            def pred(tile, c):
                rid = row + c * TK
                return jnp.where(tile == thr, jnp.where(rid < p, 1, 0), 0).astype(I32)
            return count_chunks(pred)

        def idx_body(b, p):
            cand = p | lax.shift_left(jnp.int32(1), 11 - b)
            return jnp.where(ties_before(cand) < need, cand, p)

        p_last = lax.fori_loop(0, 12, idx_body, jnp.zeros((1, TQ), I32))
        p_last = jnp.where(tie, p_last, jnp.int32(2 ** 30))

        def drop_body(c, carry):
            tile = sc_ref[pl.ds(chunk_off(c), TK), :]
            rid = row + c * TK
            dropped = jnp.where(rid > p_last, -jnp.inf, tile)
            sc_ref[pl.ds(chunk_off(c), TK), :] = jnp.where(tile == thr, dropped, tile)
            return carry

        lax.fori_loop(0, n_chunks, drop_body, 0)

    thr = thr_ref[...]

    def bias_body(c, carry):
        tile = sc_ref[pl.ds(chunk_off(c), TK), :]
        sc_ref[pl.ds(chunk_off(c), TK), :] = jnp.where(tile >= thr, 0.0, NEG)
        return carry

    lax.fori_loop(0, n_chunks, bias_body, 0)

    y_ref[...] = jnp.zeros((D_ATTN, TQ), F32)

    def att_body(c, carry, heads):
        ms, ls = carry
        off = chunk_off(c)
        new_ms, new_ls = [], []
        bias = sc_ref[pl.ds(off, TK), :]
        for n, h in enumerate(heads):
            lane0 = (h // 2) * PAIR
            q2 = qt_ref[0, h * PAIR:(h + 1) * PAIR, :]
            k2 = k_ref[0, pl.ds(off, TK), lane0:lane0 + PAIR]
            s8_ref[n] = jnp.dot(k2, q2, preferred_element_type=F32) + bias
        for n, h in enumerate(heads):
            rows_h = slice(h * HEAD_DIM, (h + 1) * HEAD_DIM)
            s = s8_ref[n]
            m_new = jnp.maximum(ms[n], jnp.max(s, axis=0, keepdims=True))
            p = jnp.exp2(s - m_new)
            alpha = jnp.exp2(ms[n] - m_new)
            new_ms.append(m_new)
            new_ls.append(alpha * ls[n] + jnp.sum(p, axis=0, keepdims=True))
            vc = vt_ref[0, rows_h, pl.ds(off, TK)]
            y_ref[rows_h, :] = alpha * y_ref[rows_h, :] + jnp.dot(
                vc, p.astype(BF16), preferred_element_type=F32)
        return tuple(new_ms), tuple(new_ls)

    ls = []
    for g in range(N_HEADS // HEADS_PER_PASS):
        heads = tuple(range(g * HEADS_PER_PASS, (g + 1) * HEADS_PER_PASS))
        init = (tuple(jnp.full((1, TQ), NEG, F32) for _ in heads),
                tuple(jnp.zeros((1, TQ), F32) for _ in heads))
        _, ls_g = lax.fori_loop(0, n_chunks, functools.partial(att_body, heads=heads), init)
        ls += list(ls_g)
    for h in range(N_HEADS):
        rows_h = slice(h * HEAD_DIM, (h + 1) * HEAD_DIM)
        out = y_ref[rows_h, :] / ls[h]
        ms_h = jnp.mean(out * out, axis=0, keepdims=True)
        y_ref[rows_h, :] = (out * lax.rsqrt(ms_h + EPS)) * gna_ref[rows_h, :]

    ya = (y_ref[...].T * gz_ref[0]).astype(BF16)
    yo = (jnp.dot(yconv_ref[0], wout_ref[0:D_CONV, :], preferred_element_type=F32)
          + jnp.dot(ya, wout_ref[D_CONV:D_CONV + D_ATTN, :], preferred_element_type=F32))
    gate = mod_ref[0][:, 2 * D_MODEL:3 * D_MODEL]
    o = x_ref[0] + gate * yo
    inv = lax.rsqrt(jnp.mean(o * o, axis=-1, keepdims=True) + EPS)
    o_ref[0] = (o * inv) * normf_ref[...]


def _attn_call(qt, qit, wt, k, ki, vt, gz, yconv, x, mod3, gn_attn_col, w_out_bf, norm_f):
    bsz, seq, _ = x.shape
    const = lambda b, i: (0, 0)
    rows = lambda b, i: (b, i, 0)
    cols = lambda b, i: (b, 0, i)
    whole = lambda b, i: (b, 0, 0)
    once = pl.Buffered(1)
    return pl.pallas_call(
        _attn_kernel,
        grid=(bsz, seq // TQ),
        in_specs=[
            pl.BlockSpec((1, N_HEADS * PAIR, TQ), cols),
            pl.BlockSpec((1, N_IDX_HEADS * PAIR, TQ), cols),
            pl.BlockSpec((1, N_IDX_HEADS, TQ), cols),
            pl.BlockSpec((1, seq, D_ATTN), whole, pipeline_mode=once),
            pl.BlockSpec((1, seq, PAIR), whole, pipeline_mode=once),
            pl.BlockSpec((1, D_ATTN, seq), whole, pipeline_mode=once),
            pl.BlockSpec((1, TQ, D_ATTN), rows),
            pl.BlockSpec((1, TQ, D_CONV), rows),
            pl.BlockSpec((1, TQ, D_MODEL), rows),
            pl.BlockSpec((1, 1, 3 * D_MODEL), whole),
            pl.BlockSpec((D_ATTN, 1), const),
            pl.BlockSpec((D_CONV + D_ATTN, D_MODEL), const, pipeline_mode=once),
            pl.BlockSpec((1, D_MODEL), const),
        ],
        out_specs=pl.BlockSpec((1, TQ, D_MODEL), rows),
        out_shape=jax.ShapeDtypeStruct((bsz, seq, D_MODEL), F32),
        scratch_shapes=[pltpu.VMEM((seq, TQ), F32), pltpu.VMEM((seq, TQ), BF16),
                        pltpu.VMEM((D_ATTN, TQ), F32),
                        pltpu.VMEM((HEADS_PER_PASS, TK, TQ), F32), pltpu.VMEM((1, TQ), F32)],
        compiler_params=pltpu.CompilerParams(dimension_semantics=("arbitrary", "arbitrary"),
                                             vmem_limit_bytes=VMEM_LIMIT),
        name="dsa_attention",
    )(qt, qit, wt, k, ki, vt, gz, yconv, x, mod3, gn_attn_col, w_out_bf, norm_f)


def _split_in_proj(w_in):
    conv = w_in[:, 0:2048]
    aq, ak, av, az = (w_in[:, 2048 + n * 512:2560 + n * 512] for n in range(4))
    iq = w_in[:, 4096:4608]
    ik = w_in[:, 4608:4672]
    iw = w_in[:, 4672:4680]
    w_nat = jnp.concatenate([conv, az], axis=1).astype(BF16)
    pad = jnp.zeros((D_MODEL, R_END - R_W - N_IDX_HEADS), w_in.dtype)
    w_cm = jnp.concatenate([aq, ak, av, iq, ik, ik, iw, pad], axis=1).T.astype(BF16)
    return w_nat, w_cm


def kernel(x, c, positions, w_ada, b_ada, norm_in, w_in, conv_w, conv_b, gn_conv, gn_attn, w_out, norm_f):
    bsz, seq, _ = x.shape
    mod3 = _ada_call(c, w_ada, b_ada).reshape(bsz, 1, 3 * D_MODEL)
    w_nat, w_cm = _split_in_proj(w_in)
    group = jnp.arange(D_CONV) // (D_CONV // CONV_GROUPS)
    gmat = jnp.where(group[:, None] == group[None, :], 1.0 / (D_CONV // CONV_GROUPS), 0.0).astype(BF16)
    yconv, gz, qt, k, vt, qit, ki, wt = _proj_call(
        x, positions.reshape(bsz, 1, seq), mod3, norm_in.reshape(1, D_MODEL), w_nat, w_cm,
        conv_w, conv_b.reshape(1, D_CONV), gn_conv.reshape(1, D_CONV), gmat)
    return _attn_call(qt, qit, wt, k, ki, vt, gz, yconv, x, mod3, gn_attn.reshape(D_ATTN, 1),
                      w_out.astype(BF16), norm_f.reshape(1, D_MODEL))
```

```python
import functools
import math

import jax
import jax.numpy as jnp
from jax import lax
from jax.experimental import pallas as pl
from jax.experimental.pallas import tpu as pltpu

D_MODEL = 1024
D_CONV = 512
D_ATTN = 512
HEAD_DIM = 64
N_HEADS = 8
CONV_GROUPS = 8
CONV_WIDTH = 3
ROPE_HALF = 8
ROPE_THETA = 500000.0
N_IDX_HEADS = 8
IDX_DIM = 64
TOPK = 256
EPS = 1e-6

LANES = 128
PAIR = 2 * HEAD_DIM
TS = 512
TQ = 512
TK = 512
HEADS_PER_PASS = 4
HALF = TQ // 2
assert TK == TQ and HALF >= TOPK
FINE_BITS = 17
NEG = -1e30
LOG2_E = math.log2(math.e)
F32_MAX = float(jnp.finfo(jnp.float32).max)
INT_MIN = -(2 ** 31)
KEY_LOWEST_FINITE = INT_MIN + 0x00800000
KEY_NEG_INF_BF16 = INT_MIN + 0x007F0000
VMEM_LIMIT = 56 * 1024 * 1024

F32 = jnp.float32
BF16 = jnp.bfloat16
I32 = jnp.int32


def _silu(v):
    return v * jax.nn.sigmoid(v)


def _ada_kernel(c_ref, w_ref, b_ref, o_ref):
    o_ref[...] = jnp.dot(_silu(c_ref[...]), w_ref[...], precision=lax.Precision.HIGHEST,
                         preferred_element_type=F32) + b_ref[...]


def _ada_call(c, w_ada, b_ada):
    bsz = c.shape[0]
    n_blk = 3
    return pl.pallas_call(
        _ada_kernel,
        grid=(n_blk,),
        in_specs=[pl.BlockSpec((bsz, D_MODEL), lambda n: (0, 0)),
                  pl.BlockSpec((D_MODEL, D_MODEL), lambda n: (0, n)),
                  pl.BlockSpec((1, D_MODEL), lambda n: (0, n))],
        out_specs=pl.BlockSpec((bsz, D_MODEL), lambda n: (0, n)),
        out_shape=jax.ShapeDtypeStruct((bsz, 3 * D_MODEL), F32),
        compiler_params=pltpu.CompilerParams(dimension_semantics=("arbitrary",)),
        name="adaln_mod",
    )(c, w_ada, b_ada.reshape(1, 3 * D_MODEL))


R_Q, R_K, R_V, R_QI, R_KI, R_W, R_END = 0, 512, 1024, 1536, 2048, 2176, 2192


def _rope_cm(xt, n_heads, cs, sn):
    t = xt.shape[-1]
    xr = xt.reshape(n_heads, HEAD_DIM, t)
    x1 = xr[:, 0:ROPE_HALF, :]
    x2 = xr[:, ROPE_HALF:2 * ROPE_HALF, :]
    r1 = x1 * cs - x2 * sn
    r2 = x1 * sn + x2 * cs
    return jnp.concatenate([r1, r2, xr[:, 2 * ROPE_HALF:, :]], axis=1).reshape(n_heads * HEAD_DIM, t)


def _pad_heads_to_pairs(xt):
    z = jnp.zeros((HEAD_DIM, xt.shape[-1]), xt.dtype)
    pieces = []
    for h in range(N_HEADS):
        blk = xt[h * HEAD_DIM:(h + 1) * HEAD_DIM, :]
        pieces += [blk, z] if h % 2 == 0 else [z, blk]
    return jnp.concatenate(pieces, axis=0)


def _proj_kernel(x_ref, pos_ref, mod_ref, normin_ref, wnat_ref, wcm_ref, convw_ref, convb_ref,
                 gnc_ref, gmat_ref,
                 yconv_ref, gz_ref, qt_ref, k_ref, vt_ref, qit_ref, ki_ref, wt_ref,
                 sbuf_ref):
    j = pl.program_id(1)
    x = x_ref[0]
    inv = lax.rsqrt(jnp.mean(x * x, axis=-1, keepdims=True) + EPS)
    mod = mod_ref[0]
    shift = mod[:, 0:D_MODEL]
    scale = mod[:, D_MODEL:2 * D_MODEL]
    h = (x * inv) * normin_ref[...] * (1.0 + scale) + shift
    hb = h.astype(BF16)

    nat = jnp.dot(hb, wnat_ref[...], preferred_element_type=F32)
    cb = nat[:, 0:512]
    cc = nat[:, 512:1024]
    cu = nat[:, 1024:1536]
    cz = nat[:, 1536:2048]
    az = nat[:, 2048:2560]
    gz_ref[0] = _silu(az)

    s = cc * cu

    @pl.when(j == 0)
    def _():
        sbuf_ref[0:8, :] = jnp.zeros((8, D_CONV), F32)

    sbuf_ref[8:TS + 8, :] = s
    s1 = sbuf_ref[7:TS + 7, :]
    s2 = sbuf_ref[6:TS + 6, :]
    cw = convw_ref[...]
    conv = convb_ref[...] + ((cw[0:1, :] * s2 + cw[1:2, :] * s1) + cw[2:3, :] * s)
    sbuf_ref[0:8, :] = sbuf_ref[TS:TS + 8, :]
    y = cb * conv
    ysq = y * y
    hi = ysq.astype(BF16)
    lo = (ysq - hi.astype(F32)).astype(BF16)
    gm = gmat_ref[...]
    ms = jnp.dot(hi, gm, preferred_element_type=F32) + jnp.dot(lo, gm, preferred_element_type=F32)
    yconv_ref[0] = ((y * lax.rsqrt(ms + EPS)) * gnc_ref[...] * _silu(cz)).astype(BF16)

    ht = h.T.astype(BF16)

    def proj_cm(r0, r1):
        return jnp.dot(wcm_ref[r0:r1, :], ht, preferred_element_type=F32)

    pos = pos_ref[0].astype(F32)
    jj = lax.broadcasted_iota(I32, (ROPE_HALF, 1), 0).astype(F32)
    theta = jnp.full((ROPE_HALF, 1), ROPE_THETA, F32)
    inv_freq = jnp.exp(jnp.log(theta) * (jj * (-1.0 / ROPE_HALF)))
    ang = inv_freq * pos
    cs = jnp.cos(ang)
    sn = jnp.sin(ang)

    scale_pow2 = HEAD_DIM ** -0.5
    q = _rope_cm(proj_cm(R_Q, R_K), N_HEADS, cs, sn) * (scale_pow2 * LOG2_E)
    qt_ref[0] = _pad_heads_to_pairs(q).astype(BF16)
    k = _rope_cm(proj_cm(R_K, R_V), N_HEADS, cs, sn)
    k_ref[0] = k.T.astype(BF16)
    vt_ref[0] = proj_cm(R_V, R_QI).astype(BF16)
    qi = _rope_cm(proj_cm(R_QI, R_KI), N_IDX_HEADS, cs, sn) * scale_pow2
    qit_ref[0] = _pad_heads_to_pairs(qi).astype(BF16)
    kiw = proj_cm(R_KI, R_END)
    ki2 = _rope_cm(kiw[0:PAIR, :], 2, cs, sn)
    ki_ref[0] = ki2.T.astype(BF16)
    wt_ref[0] = kiw[PAIR:PAIR + N_IDX_HEADS, :] * (N_IDX_HEADS ** -0.5)


def _proj_call(x, pos3, mod3, norm_in, w_nat, w_cm, conv_w, conv_b, gn_conv, gmat):
    bsz, seq, _ = x.shape
    const = lambda b, j: (0, 0)
    rows = lambda b, j: (b, j, 0)
    cols = lambda b, j: (b, 0, j)
    out_shapes = (
        jax.ShapeDtypeStruct((bsz, seq, D_CONV), BF16),
        jax.ShapeDtypeStruct((bsz, seq, D_ATTN), F32),
        jax.ShapeDtypeStruct((bsz, N_HEADS * PAIR, seq), BF16),
        jax.ShapeDtypeStruct((bsz, seq, D_ATTN), BF16),
        jax.ShapeDtypeStruct((bsz, D_ATTN, seq), BF16),
        jax.ShapeDtypeStruct((bsz, N_IDX_HEADS * PAIR, seq), BF16),
        jax.ShapeDtypeStruct((bsz, seq, PAIR), BF16),
        jax.ShapeDtypeStruct((bsz, N_IDX_HEADS, seq), F32),
    )
    return pl.pallas_call(
        _proj_kernel,
        grid=(bsz, seq // TS),
        in_specs=[
            pl.BlockSpec((1, TS, D_MODEL), rows),
            pl.BlockSpec((1, 1, TS), cols),
            pl.BlockSpec((1, 1, 3 * D_MODEL), lambda b, j: (b, 0, 0)),
            pl.BlockSpec((1, D_MODEL), const),
            pl.BlockSpec(w_nat.shape, const),
            pl.BlockSpec(w_cm.shape, const),
            pl.BlockSpec((CONV_WIDTH, D_CONV), const),
            pl.BlockSpec((1, D_CONV), const),
            pl.BlockSpec((1, D_CONV), const),
            pl.BlockSpec((D_CONV, D_CONV), const),
        ],
        out_specs=(
            pl.BlockSpec((1, TS, D_CONV), rows),
            pl.BlockSpec((1, TS, D_ATTN), rows),
            pl.BlockSpec((1, N_HEADS * PAIR, TS), cols),
            pl.BlockSpec((1, TS, D_ATTN), rows),
            pl.BlockSpec((1, D_ATTN, TS), cols),
            pl.BlockSpec((1, N_IDX_HEADS * PAIR, TS), cols),
            pl.BlockSpec((1, TS, PAIR), rows),
            pl.BlockSpec((1, N_IDX_HEADS, TS), cols),
        ),
        out_shape=out_shapes,
        scratch_shapes=[pltpu.VMEM((TS + 8, D_CONV), F32)],
        compiler_params=pltpu.CompilerParams(dimension_semantics=("arbitrary", "arbitrary"),
                                             vmem_limit_bytes=VMEM_LIMIT),
        name="in_proj",
    )(x, pos3, mod3, norm_in, w_nat, w_cm, conv_w, conv_b, gn_conv, gmat)


def _key_to_f32(key):
    bits = jnp.where(key < 0, key ^ 0x7FFFFFFF, key)
    return lax.bitcast_convert_type(bits, F32)


def _attn_kernel(qt_ref, qit_ref, wt_ref, k_ref, ki_ref, vt_ref, gz_ref, yconv_ref, x_ref, mod_ref,
                 gna_ref, wout_ref, normf_ref, o_ref, sc_ref, sb_ref, y_ref, s8_ref, thr_ref):
    i = pl.program_id(1)
    n_chunks = i + 1
    wt = wt_ref[0]
    row = lax.broadcasted_iota(I32, (TK, TQ), 0)
    col = lax.broadcasted_iota(I32, (TK, TQ), 1)

    def chunk_off(c):
        return pl.multiple_of(c * TK, TK)

    def score_chunk(c):
        kc = ki_ref[0, pl.ds(chunk_off(c), TK), :]
        acc = jnp.zeros((TK, TQ), F32)
        for hh in range(N_IDX_HEADS):
            d = jnp.dot(kc, qit_ref[0, hh * PAIR:(hh + 1) * PAIR, :], preferred_element_type=F32)
            acc = acc + wt[hh:hh + 1, :] * jnp.maximum(d, 0.0)
        return acc

    def put_scores(c, tile):
        sc_ref[pl.ds(chunk_off(c), TK), :] = tile
        sb_ref[pl.ds(chunk_off(c), TK), :] = tile.astype(BF16)

    def score_body(c, carry):
        put_scores(c, score_chunk(c))
        return carry

    lax.fori_loop(0, i, score_body, 0)
    put_scores(i, jnp.where(row <= col, score_chunk(i), -jnp.inf))

    def count_chunks(pred):
        def body(c, acc):
            tile = sc_ref[pl.ds(chunk_off(c), TK), :]
            return acc + jnp.sum(pred(tile, c).reshape(TK // 8, 8, TQ), axis=0)
        acc = lax.fori_loop(0, n_chunks, body, jnp.zeros((8, TQ), I32))
        return jnp.sum(acc, axis=0, keepdims=True)

    def swept_counts(src_ref, rows_per_tile, tile_hits):
        def body(c, acc):
            return acc + tile_hits(src_ref[pl.ds(chunk_off(c), TK), :], slice(None))
        acc = lax.fori_loop(0, i, body, jnp.zeros((rows_per_tile, TQ), F32))
        d0 = chunk_off(i)
        acc = acc + tile_hits(src_ref[pl.ds(d0, HALF), :], slice(None))
        right = tile_hits(src_ref[pl.ds(pl.multiple_of(d0 + HALF, HALF), HALF), HALF:],
                          slice(HALF, None))
        return acc + jnp.concatenate([jnp.zeros((rows_per_tile, HALF), F32), right], axis=1)

    def count_ge(thr):
        def tile_hits(tile, lanes):
            hit = jnp.where(tile >= thr[:, lanes], 1.0, 0.0)
            return jnp.sum(hit.reshape(tile.shape[0] // 8, 8, tile.shape[1]), axis=0)
        return jnp.sum(swept_counts(sc_ref, 8, tile_hits), axis=0, keepdims=True).astype(I32)

    def count_ge_bf16(thr_b):
        def tile_hits(tile, lanes):
            hit = jnp.where(tile >= thr_b[:, lanes], jnp.ones((), BF16), jnp.zeros((), BF16))
            hit = hit.reshape(tile.shape[0] // 16, 16, tile.shape[1])
            part = hit[0]
            for r in range(1, tile.shape[0] // 16):
                part = part + hit[r]
            return part.astype(F32)
        return jnp.sum(swept_counts(sb_ref, 16, tile_hits), axis=0, keepdims=True).astype(I32)

    def coarse_body(b, lo_u):
        cand_u = lo_u | lax.shift_left(jnp.int32(1), 31 - b)
        key = cand_u ^ INT_MIN
        bits = jnp.where(key < 0, key ^ 0x7FFF0000, key)
        cand = lax.bitcast_convert_type(bits, F32).astype(BF16)
        cnt = jnp.where(key < KEY_NEG_INF_BF16, TOPK, count_ge_bf16(cand))
        return jnp.where(cnt >= TOPK, cand_u, lo_u)

    coarse_u = lax.fori_loop(0, 16, coarse_body, jnp.zeros((1, TQ), I32))
    coarse_key = coarse_u ^ INT_MIN
    coarse_key = jnp.where(coarse_key < 0, coarse_key | 0xFFFF, coarse_key)
    base_key = coarse_key - (1 << (FINE_BITS - 1))

    def fine_body(b, carry):
        off, cnt_lo = carry
        cand_off = off | jnp.where(b == 0, 0, lax.shift_left(jnp.int32(1), FINE_BITS - b))
        cnt = count_ge(_key_to_f32(base_key + cand_off))
        ok = jnp.where(b == 0, TOPK, cnt) >= TOPK
        return jnp.where(ok, cand_off, off), jnp.where(ok, cnt, cnt_lo)

    off, cnt_lo = lax.fori_loop(0, FINE_BITS + 1, fine_body,
                                (jnp.zeros((1, TQ), I32), jnp.zeros((1, TQ), I32)))
    lo = base_key + off
    has_kth = lo >= KEY_LOWEST_FINITE
    thr0 = jnp.where(has_kth, _key_to_f32(lo), -F32_MAX)
    tie0 = jnp.where(has_kth, cnt_lo, 0) > TOPK

    thr_ref[...] = thr0

    @pl.when(jnp.max(tie0.astype(I32)) > 0)
    def _():
        def midpoint(lo_f, hi_f):
            return lo_f + (hi_f - lo_f) * 0.5

        def splits(lo_f, hi_f):
            mid = midpoint(lo_f, hi_f)
            return jnp.where(mid > lo_f, jnp.where(mid < hi_f, 1, 0), 0).astype(I32)

        def refine_cond(carry):
            lo_f, hi_f, _, it = carry
            return jnp.logical_and(jnp.max(splits(lo_f, hi_f)) > 0, it < 64)

        def refine_body(carry):
            lo_f, hi_f, cnt_f, it = carry
            mid = midpoint(lo_f, hi_f)
            cnt = count_ge(mid)
            up = jnp.where(splits(lo_f, hi_f) > 0, jnp.where(cnt >= TOPK, 1, 0), 0) > 0
            down = jnp.where(splits(lo_f, hi_f) > 0, jnp.where(cnt >= TOPK, 0, 1), 0) > 0
            return (jnp.where(up, mid, lo_f), jnp.where(down, mid, hi_f),
                    jnp.where(up, cnt, cnt_f), it + 1)

        above0 = jnp.where(tie0, _key_to_f32(lo + 1), thr0)
        thr, _, cnt_thr, _ = lax.while_loop(refine_cond, refine_body,
                                            (thr0, above0, cnt_lo, jnp.int32(0)))
        thr_ref[...] = thr

        tie = jnp.where(tie0, cnt_thr, 0) > TOPK
        need = TOPK - count_chunks(lambda tile, c: (tile > thr).astype(I32))

        def ties_before(p):
            def pred(tile, c):
                rid = row + c * TK
                return jnp.where(tile == thr, jnp.where(rid < p, 1, 0), 0).astype(I32)
            return count_chunks(pred)

        def idx_body(b, p):
            cand = p | lax.shift_left(jnp.int32(1), 11 - b)
            return jnp.where(ties_before(cand) < need, cand, p)

        p_last = lax.fori_loop(0, 12, idx_body, jnp.zeros((1, TQ), I32))
        p_last = jnp.where(tie, p_last, jnp.int32(2 ** 30))

        def drop_body(c, carry):
            tile = sc_ref[pl.ds(chunk_off(c), TK), :]
            rid = row + c * TK
            dropped = jnp.where(rid > p_last, -jnp.inf, tile)
            sc_ref[pl.ds(chunk_off(c), TK), :] = jnp.where(tile == thr, dropped, tile)
            return carry

        lax.fori_loop(0, n_chunks, drop_body, 0)

    thr = thr_ref[...]

    def bias_body(c, carry):
        tile = sc_ref[pl.ds(chunk_off(c), TK), :]
        sc_ref[pl.ds(chunk_off(c), TK), :] = jnp.where(tile >= thr, 0.0, NEG)
        return carry

    lax.fori_loop(0, n_chunks, bias_body, 0)

    y_ref[...] = jnp.zeros((D_ATTN, TQ), F32)

    def att_body(c, carry, heads):
        ms, ls = carry
        off = chunk_off(c)
        new_ms, new_ls = [], []
        bias = sc_ref[pl.ds(off, TK), :]
        for n, h in enumerate(heads):
            lane0 = (h // 2) * PAIR
            q2 = qt_ref[0, h * PAIR:(h + 1) * PAIR, :]
            k2 = k_ref[0, pl.ds(off, TK), lane0:lane0 + PAIR]
            s8_ref[n] = jnp.dot(k2, q2, preferred_element_type=F32) + bias
        for n, h in enumerate(heads):
            rows_h = slice(h * HEAD_DIM, (h + 1) * HEAD_DIM)
            s = s8_ref[n]
            m_new = jnp.maximum(ms[n], jnp.max(s, axis=0, keepdims=True))
            p = jnp.exp2(s - m_new)
            alpha = jnp.exp2(ms[n] - m_new)
            new_ms.append(m_new)
            new_ls.append(alpha * ls[n] + jnp.sum(p, axis=0, keepdims=True))
            vc = vt_ref[0, rows_h, pl.ds(off, TK)]
            y_ref[rows_h, :] = alpha * y_ref[rows_h, :] + jnp.dot(
                vc, p.astype(BF16), preferred_element_type=F32)
        return tuple(new_ms), tuple(new_ls)

    ls = []
    for g in range(N_HEADS // HEADS_PER_PASS):
        heads = tuple(range(g * HEADS_PER_PASS, (g + 1) * HEADS_PER_PASS))
        init = (tuple(jnp.full((1, TQ), NEG, F32) for _ in heads),
                tuple(jnp.zeros((1, TQ), F32) for _ in heads))
        _, ls_g = lax.fori_loop(0, n_chunks, functools.partial(att_body, heads=heads), init)
        ls += list(ls_g)
    for h in range(N_HEADS):
        rows_h = slice(h * HEAD_DIM, (h + 1) * HEAD_DIM)
        out = y_ref[rows_h, :] / ls[h]
        ms_h = jnp.mean(out * out, axis=0, keepdims=True)
        y_ref[rows_h, :] = (out * lax.rsqrt(ms_h + EPS)) * gna_ref[rows_h, :]

    ya = (y_ref[...].T * gz_ref[0]).astype(BF16)
    yo = (jnp.dot(yconv_ref[0], wout_ref[0:D_CONV, :], preferred_element_type=F32)
          + jnp.dot(ya, wout_ref[D_CONV:D_CONV + D_ATTN, :], preferred_element_type=F32))
    gate = mod_ref[0][:, 2 * D_MODEL:3 * D_MODEL]
    o = x_ref[0] + gate * yo
    inv = lax.rsqrt(jnp.mean(o * o, axis=-1, keepdims=True) + EPS)
    o_ref[0] = (o * inv) * normf_ref[...]


def _attn_call(qt, qit, wt, k, ki, vt, gz, yconv, x, mod3, gn_attn_col, w_out_bf, norm_f):
    bsz, seq, _ = x.shape
    const = lambda b, i: (0, 0)
    rows = lambda b, i: (b, i, 0)
    cols = lambda b, i: (b, 0, i)
    whole = lambda b, i: (b, 0, 0)
    once = pl.Buffered(1)
    return pl.pallas_call(
        _attn_kernel,
        grid=(bsz, seq // TQ),
        in_specs=[
            pl.BlockSpec((1, N_HEADS * PAIR, TQ), cols),
            pl.BlockSpec((1, N_IDX_HEADS * PAIR, TQ), cols),
            pl.BlockSpec((1, N_IDX_HEADS, TQ), cols),
            pl.BlockSpec((1, seq, D_ATTN), whole, pipeline_mode=once),
            pl.BlockSpec((1, seq, PAIR), whole, pipeline_mode=once),
            pl.BlockSpec((1, D_ATTN, seq), whole, pipeline_mode=once),
            pl.BlockSpec((1, TQ, D_ATTN), rows),
            pl.BlockSpec((1, TQ, D_CONV), rows),
            pl.BlockSpec((1, TQ, D_MODEL), rows),
            pl.BlockSpec((1, 1, 3 * D_MODEL), whole),
            pl.BlockSpec((D_ATTN, 1), const),
            pl.BlockSpec((D_CONV + D_ATTN, D_MODEL), const, pipeline_mode=once),
            pl.BlockSpec((1, D_MODEL), const),
        ],
        out_specs=pl.BlockSpec((1, TQ, D_MODEL), rows),
        out_shape=jax.ShapeDtypeStruct((bsz, seq, D_MODEL), F32),
        scratch_shapes=[pltpu.VMEM((seq, TQ), F32), pltpu.VMEM((seq, TQ), BF16),
                        pltpu.VMEM((D_ATTN, TQ), F32),
                        pltpu.VMEM((HEADS_PER_PASS, TK, TQ), F32), pltpu.VMEM((1, TQ), F32)],
        compiler_params=pltpu.CompilerParams(dimension_semantics=("arbitrary", "arbitrary"),
                                             vmem_limit_bytes=VMEM_LIMIT),
        name="dsa_attention",
    )(qt, qit, wt, k, ki, vt, gz, yconv, x, mod3, gn_attn_col, w_out_bf, norm_f)


def _split_in_proj(w_in):
    conv = w_in[:, 0:2048]
    aq, ak, av, az = (w_in[:, 2048 + n * 512:2560 + n * 512] for n in range(4))
    iq = w_in[:, 4096:4608]
    ik = w_in[:, 4608:4672]
    iw = w_in[:, 4672:4680]
    w_nat = jnp.concatenate([conv, az], axis=1).astype(BF16)
    pad = jnp.zeros((D_MODEL, R_END - R_W - N_IDX_HEADS), w_in.dtype)
    w_cm = jnp.concatenate([aq, ak, av, iq, ik, ik, iw, pad], axis=1).T.astype(BF16)
    return w_nat, w_cm


def kernel(x, c, positions, w_ada, b_ada, norm_in, w_in, conv_w, conv_b, gn_conv, gn_attn, w_out, norm_f):
    bsz, seq, _ = x.shape
    mod3 = _ada_call(c, w_ada, b_ada).reshape(bsz, 1, 3 * D_MODEL)
    w_nat, w_cm = _split_in_proj(w_in)
    group = jnp.arange(D_CONV) // (D_CONV // CONV_GROUPS)
    gmat = jnp.where(group[:, None] == group[None, :], 1.0 / (D_CONV // CONV_GROUPS), 0.0).astype(BF16)
    yconv, gz, qt, k, vt, qit, ki, wt = _proj_call(
        x, positions.reshape(bsz, 1, seq), mod3, norm_in.reshape(1, D_MODEL), w_nat, w_cm,
        conv_w, conv_b.reshape(1, D_CONV), gn_conv.reshape(1, D_CONV), gmat)
    return _attn_call(qt, qit, wt, k, ki, vt, gz, yconv, x, mod3, gn_attn.reshape(D_ATTN, 1),
                      w_out.astype(BF16), norm_f.reshape(1, D_MODEL))
```

```python
import functools
import math

import jax
import jax.numpy as jnp
from jax import lax
from jax.experimental import pallas as pl
from jax.experimental.pallas import tpu as pltpu

D_MODEL = 1024
D_CONV = 512
D_ATTN = 512
HEAD_DIM = 64
N_HEADS = 8
CONV_GROUPS = 8
CONV_WIDTH = 3
ROPE_HALF = 8
ROPE_THETA = 500000.0
N_IDX_HEADS = 8
IDX_DIM = 64
TOPK = 256
EPS = 1e-6

LANES = 128
PAIR = 2 * HEAD_DIM
TS = 512
TQ = 512
TK = 512
HEADS_PER_PASS = 4
HALF = TQ // 2
assert TK == TQ and HALF >= TOPK
FINE_BITS = 17
NEG = -1e30
LOG2_E = math.log2(math.e)
F32_MAX = float(jnp.finfo(jnp.float32).max)
INT_MIN = -(2 ** 31)
KEY_LOWEST_FINITE = INT_MIN + 0x00800000
KEY_NEG_INF_BF16 = INT_MIN + 0x007F0000
VMEM_LIMIT = 56 * 1024 * 1024

F32 = jnp.float32
BF16 = jnp.bfloat16
I32 = jnp.int32


def _silu(v):
    return v * jax.nn.sigmoid(v)


def _ada_kernel(c_ref, w_ref, b_ref, o_ref):
    o_ref[...] = jnp.dot(_silu(c_ref[...]), w_ref[...], precision=lax.Precision.HIGHEST,
                         preferred_element_type=F32) + b_ref[...]


def _ada_call(c, w_ada, b_ada):
    bsz = c.shape[0]
    n_blk = 3
    return pl.pallas_call(
        _ada_kernel,
        grid=(n_blk,),
        in_specs=[pl.BlockSpec((bsz, D_MODEL), lambda n: (0, 0)),
                  pl.BlockSpec((D_MODEL, D_MODEL), lambda n: (0, n)),
                  pl.BlockSpec((1, D_MODEL), lambda n: (0, n))],
        out_specs=pl.BlockSpec((bsz, D_MODEL), lambda n: (0, n)),
        out_shape=jax.ShapeDtypeStruct((bsz, 3 * D_MODEL), F32),
        compiler_params=pltpu.CompilerParams(dimension_semantics=("arbitrary",)),
        name="adaln_mod",
    )(c, w_ada, b_ada.reshape(1, 3 * D_MODEL))


R_Q, R_K, R_V, R_QI, R_KI, R_W, R_END = 0, 512, 1024, 1536, 2048, 2176, 2192


def _rope_cm(xt, n_heads, cs, sn):
    t = xt.shape[-1]
    xr = xt.reshape(n_heads, HEAD_DIM, t)
    x1 = xr[:, 0:ROPE_HALF, :]
    x2 = xr[:, ROPE_HALF:2 * ROPE_HALF, :]
    r1 = x1 * cs - x2 * sn
    r2 = x1 * sn + x2 * cs
    return jnp.concatenate([r1, r2, xr[:, 2 * ROPE_HALF:, :]], axis=1).reshape(n_heads * HEAD_DIM, t)


def _pad_heads_to_pairs(xt):
    z = jnp.zeros((HEAD_DIM, xt.shape[-1]), xt.dtype)
    pieces = []
    for h in range(N_HEADS):
        blk = xt[h * HEAD_DIM:(h + 1) * HEAD_DIM, :]
        pieces += [blk, z] if h % 2 == 0 else [z, blk]
    return jnp.concatenate(pieces, axis=0)


def _proj_kernel(x_ref, pos_ref, mod_ref, normin_ref, wnat_ref, wcm_ref, convw_ref, convb_ref,
                 gnc_ref, gmat_ref,
                 yconv_ref, gz_ref, qt_ref, k_ref, vt_ref, qit_ref, ki_ref, wt_ref,
                 sbuf_ref):
    j = pl.program_id(1)
    x = x_ref[0]
    inv = lax.rsqrt(jnp.mean(x * x, axis=-1, keepdims=True) + EPS)
    mod = mod_ref[0]
    shift = mod[:, 0:D_MODEL]
    scale = mod[:, D_MODEL:2 * D_MODEL]
    h = (x * inv) * normin_ref[...] * (1.0 + scale) + shift
    hb = h.astype(BF16)

    nat = jnp.dot(hb, wnat_ref[...], preferred_element_type=F32)
    cb = nat[:, 0:512]
    cc = nat[:, 512:1024]
    cu = nat[:, 1024:1536]
    cz = nat[:, 1536:2048]
    az = nat[:, 2048:2560]
    gz_ref[0] = _silu(az)

    s = cc * cu

    @pl.when(j == 0)
    def _():
        sbuf_ref[0:8, :] = jnp.zeros((8, D_CONV), F32)

    sbuf_ref[8:TS + 8, :] = s
    s1 = sbuf_ref[7:TS + 7, :]
    s2 = sbuf_ref[6:TS + 6, :]
    cw = convw_ref[...]
    conv = convb_ref[...] + ((cw[0:1, :] * s2 + cw[1:2, :] * s1) + cw[2:3, :] * s)
    sbuf_ref[0:8, :] = sbuf_ref[TS:TS + 8, :]
    y = cb * conv
    ysq = y * y
    hi = ysq.astype(BF16)
    lo = (ysq - hi.astype(F32)).astype(BF16)
    gm = gmat_ref[...]
    ms = jnp.dot(hi, gm, preferred_element_type=F32) + jnp.dot(lo, gm, preferred_element_type=F32)
    yconv_ref[0] = ((y * lax.rsqrt(ms + EPS)) * gnc_ref[...] * _silu(cz)).astype(BF16)

    ht = h.T.astype(BF16)

    def proj_cm(r0, r1):
        return jnp.dot(wcm_ref[r0:r1, :], ht, preferred_element_type=F32)

    pos = pos_ref[0].astype(F32)
    jj = lax.broadcasted_iota(I32, (ROPE_HALF, 1), 0).astype(F32)
    theta = jnp.full((ROPE_HALF, 1), ROPE_THETA, F32)
    inv_freq = jnp.exp(jnp.log(theta) * (jj * (-1.0 / ROPE_HALF)))
    ang = inv_freq * pos
    cs = jnp.cos(ang)
    sn = jnp.sin(ang)

    scale_pow2 = HEAD_DIM ** -0.5
    q = _rope_cm(proj_cm(R_Q, R_K), N_HEADS, cs, sn) * (scale_pow2 * LOG2_E)
    qt_ref[0] = _pad_heads_to_pairs(q).astype(BF16)
    k = _rope_cm(proj_cm(R_K, R_V), N_HEADS, cs, sn)
    k_ref[0] = k.T.astype(BF16)
    vt_ref[0] = proj_cm(R_V, R_QI).astype(BF16)
    qi = _rope_cm(proj_cm(R_QI, R_KI), N_IDX_HEADS, cs, sn) * scale_pow2
    qit_ref[0] = _pad_heads_to_pairs(qi).astype(BF16)
    kiw = proj_cm(R_KI, R_END)
    ki2 = _rope_cm(kiw[0:PAIR, :], 2, cs, sn)
    ki_ref[0] = ki2.T.astype(BF16)
    wt_ref[0] = kiw[PAIR:PAIR + N_IDX_HEADS, :] * (N_IDX_HEADS ** -0.5)


def _proj_call(x, pos3, mod3, norm_in, w_nat, w_cm, conv_w, conv_b, gn_conv, gmat):
    bsz, seq, _ = x.shape
    const = lambda b, j: (0, 0)
    rows = lambda b, j: (b, j, 0)
    cols = lambda b, j: (b, 0, j)
    out_shapes = (
        jax.ShapeDtypeStruct((bsz, seq, D_CONV), BF16),
        jax.ShapeDtypeStruct((bsz, seq, D_ATTN), F32),
        jax.ShapeDtypeStruct((bsz, N_HEADS * PAIR, seq), BF16),
        jax.ShapeDtypeStruct((bsz, seq, D_ATTN), BF16),
        jax.ShapeDtypeStruct((bsz, D_ATTN, seq), BF16),
        jax.ShapeDtypeStruct((bsz, N_IDX_HEADS * PAIR, seq), BF16),
        jax.ShapeDtypeStruct((bsz, seq, PAIR), BF16),
        jax.ShapeDtypeStruct((bsz, N_IDX_HEADS, seq), F32),
    )
    return pl.pallas_call(
        _proj_kernel,
        grid=(bsz, seq // TS),
        in_specs=[
            pl.BlockSpec((1, TS, D_MODEL), rows),
            pl.BlockSpec((1, 1, TS), cols),
            pl.BlockSpec((1, 1, 3 * D_MODEL), lambda b, j: (b, 0, 0)),
            pl.BlockSpec((1, D_MODEL), const),
            pl.BlockSpec(w_nat.shape, const),
            pl.BlockSpec(w_cm.shape, const),
            pl.BlockSpec((CONV_WIDTH, D_CONV), const),
            pl.BlockSpec((1, D_CONV), const),
            pl.BlockSpec((1, D_CONV), const),
            pl.BlockSpec((D_CONV, D_CONV), const),
        ],
        out_specs=(
            pl.BlockSpec((1, TS, D_CONV), rows),
            pl.BlockSpec((1, TS, D_ATTN), rows),
            pl.BlockSpec((1, N_HEADS * PAIR, TS), cols),
            pl.BlockSpec((1, TS, D_ATTN), rows),
            pl.BlockSpec((1, D_ATTN, TS), cols),
            pl.BlockSpec((1, N_IDX_HEADS * PAIR, TS), cols),
            pl.BlockSpec((1, TS, PAIR), rows),
            pl.BlockSpec((1, N_IDX_HEADS, TS), cols),
        ),
        out_shape=out_shapes,
        scratch_shapes=[pltpu.VMEM((TS + 8, D_CONV), F32)],
        compiler_params=pltpu.CompilerParams(dimension_semantics=("arbitrary", "arbitrary"),
                                             vmem_limit_bytes=VMEM_LIMIT),
        name="in_proj",
    )(x, pos3, mod3, norm_in, w_nat, w_cm, conv_w, conv_b, gn_conv, gmat)


def _key_to_f32(key):
    bits = jnp.where(key < 0, key ^ 0x7FFFFFFF, key)
    return lax.bitcast_convert_type(bits, F32)


def _attn_kernel(qt_ref, qit_ref, wt_ref, k_ref, ki_ref, vt_ref, gz_ref, yconv_ref, x_ref, mod_ref,
                 gna_ref, wout_ref, normf_ref, o_ref, sc_ref, sb_ref, y_ref, s8_ref, thr_ref):
    i = pl.program_id(1)
    n_chunks = i + 1
    wt = wt_ref[0]
    row = lax.broadcasted_iota(I32, (TK, TQ), 0)
    col = lax.broadcasted_iota(I32, (TK, TQ), 1)

    def chunk_off(c):
        return pl.multiple_of(c * TK, TK)

    def score_chunk(c):
        kc = ki_ref[0, pl.ds(chunk_off(c), TK), :]
        acc = jnp.zeros((TK, TQ), F32)
        for hh in range(N_IDX_HEADS):
            d = jnp.dot(kc, qit_ref[0, hh * PAIR:(hh + 1) * PAIR, :], preferred_element_type=F32)
            acc = acc + wt[hh:hh + 1, :] * jnp.maximum(d, 0.0)
        return acc

    def put_scores(c, tile):
        sc_ref[pl.ds(chunk_off(c), TK), :] = tile
        sb_ref[pl.ds(chunk_off(c), TK), :] = tile.astype(BF16)

    def score_body(c, carry):
        put_scores(c, score_chunk(c))
        return carry

    lax.fori_loop(0, i, score_body, 0)
    put_scores(i, jnp.where(row <= col, score_chunk(i), -jnp.inf))

    def count_chunks(pred):
        def body(c, acc):
            tile = sc_ref[pl.ds(chunk_off(c), TK), :]
            return acc + jnp.sum(pred(tile, c).reshape(TK // 8, 8, TQ), axis=0)
        acc = lax.fori_loop(0, n_chunks, body, jnp.zeros((8, TQ), I32))
        return jnp.sum(acc, axis=0, keepdims=True)

    def swept_counts(src_ref, rows_per_tile, tile_hits):
        def body(c, acc):
            return acc + tile_hits(src_ref[pl.ds(chunk_off(c), TK), :], slice(None))
        acc = lax.fori_loop(0, i, body, jnp.zeros((rows_per_tile, TQ), F32))
        d0 = chunk_off(i)
        acc = acc + tile_hits(src_ref[pl.ds(d0, HALF), :], slice(None))
        right = tile_hits(src_ref[pl.ds(pl.multiple_of(d0 + HALF, HALF), HALF), HALF:],
                          slice(HALF, None))
        return acc + jnp.concatenate([jnp.zeros((rows_per_tile, HALF), F32), right], axis=1)

    def count_ge(thr):
        def tile_hits(tile, lanes):
            hit = jnp.where(tile >= thr[:, lanes], 1.0, 0.0)
            return jnp.sum(hit.reshape(tile.shape[0] // 8, 8, tile.shape[1]), axis=0)
        return jnp.sum(swept_counts(sc_ref, 8, tile_hits), axis=0, keepdims=True).astype(I32)

    def count_ge_bf16(thr_b):
        def tile_hits(tile, lanes):
            hit = jnp.where(tile >= thr_b[:, lanes], jnp.ones((), BF16), jnp.zeros((), BF16))
            hit = hit.reshape(tile.shape[0] // 16, 16, tile.shape[1])
            part = hit[0]
            for r in range(1, tile.shape[0] // 16):
                part = part + hit[r]
            return part.astype(F32)
        return jnp.sum(swept_counts(sb_ref, 16, tile_hits), axis=0, keepdims=True).astype(I32)

    def coarse_body(b, lo_u):
        cand_u = lo_u | lax.shift_left(jnp.int32(1), 31 - b)
        key = cand_u ^ INT_MIN
        bits = jnp.where(key < 0, key ^ 0x7FFF0000, key)
        cand = lax.bitcast_convert_type(bits, F32).astype(BF16)
        cnt = jnp.where(key < KEY_NEG_INF_BF16, TOPK, count_ge_bf16(cand))
        return jnp.where(cnt >= TOPK, cand_u, lo_u)

    coarse_u = lax.fori_loop(0, 16, coarse_body, jnp.zeros((1, TQ), I32))
    coarse_key = coarse_u ^ INT_MIN
    coarse_key = jnp.where(coarse_key < 0, coarse_key | 0xFFFF, coarse_key)
    base_key = coarse_key - (1 << (FINE_BITS - 1))

    def fine_body(b, carry):
        off, cnt_lo = carry
        cand_off = off | jnp.where(b == 0, 0, lax.shift_left(jnp.int32(1), FINE_BITS - b))
        cnt = count_ge(_key_to_f32(base_key + cand_off))
        ok = jnp.where(b == 0, TOPK, cnt) >= TOPK
        return jnp.where(ok, cand_off, off), jnp.where(ok, cnt, cnt_lo)

    off, cnt_lo = lax.fori_loop(0, FINE_BITS + 1, fine_body,
                                (jnp.zeros((1, TQ), I32), jnp.zeros((1, TQ), I32)))
    lo = base_key + off
    has_kth = lo >= KEY_LOWEST_FINITE
    thr0 = jnp.where(has_kth, _key_to_f32(lo), -F32_MAX)
    tie0 = jnp.where(has_kth, cnt_lo, 0) > TOPK

    thr_ref[...] = thr0

    @pl.when(jnp.max(tie0.astype(I32)) > 0)
    def _():
        def midpoint(lo_f, hi_f):
            return lo_f + (hi_f - lo_f) * 0.5

        def splits(lo_f, hi_f):
            mid = midpoint(lo_f, hi_f)
            return jnp.where(mid > lo_f, jnp.where(mid < hi_f, 1, 0), 0).astype(I32)

        def refine_cond(carry):
            lo_f, hi_f, _, it = carry
            return jnp.logical_and(jnp.max(splits(lo_f, hi_f)) > 0, it < 64)

        def refine_body(carry):
            lo_f, hi_f, cnt_f, it = carry
            mid = midpoint(lo_f, hi_f)
            cnt = count_ge(mid)
            up = jnp.where(splits(lo_f, hi_f) > 0, jnp.where(cnt >= TOPK, 1, 0), 0) > 0
            down = jnp.where(splits(lo_f, hi_f) > 0, jnp.where(cnt >= TOPK, 0, 1), 0) > 0
            return (jnp.where(up, mid, lo_f), jnp.where(down, mid, hi_f),
                    jnp.where(up, cnt, cnt_f), it + 1)

        above0 = jnp.where(tie0, _key_to_f32(lo + 1), thr0)
        thr, _, cnt_thr, _ = lax.while_loop(refine_cond, refine_body,
                                            (thr0, above0, cnt_lo, jnp.int32(0)))
        thr_ref[...] = thr

        tie = jnp.where(tie0, cnt_thr, 0) > TOPK
        need = TOPK - count_chunks(lambda tile, c: (tile > thr).astype(I32))

        keep = jnp.where(tie, need, jnp.int32(2 ** 30)).astype(F32)
        prefix = jnp.where(row >= col, 1.0, 0.0).astype(BF16)

        def drop_body(c, seen):
            tile = sc_ref[pl.ds(chunk_off(c), TK), :]
            tied = tile == thr
            rank = seen + jnp.dot(prefix, jnp.where(tied, 1.0, 0.0).astype(BF16),
                                  preferred_element_type=F32)
            dropped = jnp.where(rank > keep, -jnp.inf, tile)
            sc_ref[pl.ds(chunk_off(c), TK), :] = jnp.where(tied, dropped, tile)
            return rank[TK - 1:TK, :]

        lax.fori_loop(0, n_chunks, drop_body, jnp.zeros((1, TQ), F32))

    thr = thr_ref[...]

    def bias_body(c, carry):
        tile = sc_ref[pl.ds(chunk_off(c), TK), :]
        sc_ref[pl.ds(chunk_off(c), TK), :] = jnp.where(tile >= thr, 0.0, NEG)
        return carry

    lax.fori_loop(0, n_chunks, bias_body, 0)

    y_ref[...] = jnp.zeros((D_ATTN, TQ), F32)

    def att_body(c, carry, heads):
        ms, ls = carry
        off = chunk_off(c)
        new_ms, new_ls = [], []
        bias = sc_ref[pl.ds(off, TK), :]
        for n, h in enumerate(heads):
            lane0 = (h // 2) * PAIR
            q2 = qt_ref[0, h * PAIR:(h + 1) * PAIR, :]
            k2 = k_ref[0, pl.ds(off, TK), lane0:lane0 + PAIR]
            s8_ref[n] = jnp.dot(k2, q2, preferred_element_type=F32) + bias
        for n, h in enumerate(heads):
            rows_h = slice(h * HEAD_DIM, (h + 1) * HEAD_DIM)
            s = s8_ref[n]
            m_new = jnp.maximum(ms[n], jnp.max(s, axis=0, keepdims=True))
            p = jnp.exp2(s - m_new)
            alpha = jnp.exp2(ms[n] - m_new)
            new_ms.append(m_new)
            new_ls.append(alpha * ls[n] + jnp.sum(p, axis=0, keepdims=True))
            vc = vt_ref[0, rows_h, pl.ds(off, TK)]
            y_ref[rows_h, :] = alpha * y_ref[rows_h, :] + jnp.dot(
                vc, p.astype(BF16), preferred_element_type=F32)
        return tuple(new_ms), tuple(new_ls)

    ls = []
    for g in range(N_HEADS // HEADS_PER_PASS):
        heads = tuple(range(g * HEADS_PER_PASS, (g + 1) * HEADS_PER_PASS))
        init = (tuple(jnp.full((1, TQ), NEG, F32) for _ in heads),
                tuple(jnp.zeros((1, TQ), F32) for _ in heads))
        _, ls_g = lax.fori_loop(0, n_chunks, functools.partial(att_body, heads=heads), init)
        ls += list(ls_g)
    for h in range(N_HEADS):
        rows_h = slice(h * HEAD_DIM, (h + 1) * HEAD_DIM)
        out = y_ref[rows_h, :] / ls[h]
        ms_h = jnp.mean(out * out, axis=0, keepdims=True)
        y_ref[rows_h, :] = (out * lax.rsqrt(ms_h + EPS)) * gna_ref[rows_h, :]

    ya = (y_ref[...].T * gz_ref[0]).astype(BF16)
    yo = (jnp.dot(yconv_ref[0], wout_ref[0:D_CONV, :], preferred_element_type=F32)
          + jnp.dot(ya, wout_ref[D_CONV:D_CONV + D_ATTN, :], preferred_element_type=F32))
    gate = mod_ref[0][:, 2 * D_MODEL:3 * D_MODEL]
    o = x_ref[0] + gate * yo
    inv = lax.rsqrt(jnp.mean(o * o, axis=-1, keepdims=True) + EPS)
    o_ref[0] = (o * inv) * normf_ref[...]


def _attn_call(qt, qit, wt, k, ki, vt, gz, yconv, x, mod3, gn_attn_col, w_out_bf, norm_f):
    bsz, seq, _ = x.shape
    const = lambda b, i: (0, 0)
    rows = lambda b, i: (b, i, 0)
    cols = lambda b, i: (b, 0, i)
    whole = lambda b, i: (b, 0, 0)
    once = pl.Buffered(1)
    return pl.pallas_call(
        _attn_kernel,
        grid=(bsz, seq // TQ),
        in_specs=[
            pl.BlockSpec((1, N_HEADS * PAIR, TQ), cols),
            pl.BlockSpec((1, N_IDX_HEADS * PAIR, TQ), cols),
            pl.BlockSpec((1, N_IDX_HEADS, TQ), cols),
            pl.BlockSpec((1, seq, D_ATTN), whole, pipeline_mode=once),
            pl.BlockSpec((1, seq, PAIR), whole, pipeline_mode=once),
            pl.BlockSpec((1, D_ATTN, seq), whole, pipeline_mode=once),
            pl.BlockSpec((1, TQ, D_ATTN), rows),
            pl.BlockSpec((1, TQ, D_CONV), rows),
            pl.BlockSpec((1, TQ, D_MODEL), rows),
            pl.BlockSpec((1, 1, 3 * D_MODEL), whole),
            pl.BlockSpec((D_ATTN, 1), const),
            pl.BlockSpec((D_CONV + D_ATTN, D_MODEL), const, pipeline_mode=once),
            pl.BlockSpec((1, D_MODEL), const),
        ],
        out_specs=pl.BlockSpec((1, TQ, D_MODEL), rows),
        out_shape=jax.ShapeDtypeStruct((bsz, seq, D_MODEL), F32),
        scratch_shapes=[pltpu.VMEM((seq, TQ), F32), pltpu.VMEM((seq, TQ), BF16),
                        pltpu.VMEM((D_ATTN, TQ), F32),
                        pltpu.VMEM((HEADS_PER_PASS, TK, TQ), F32), pltpu.VMEM((1, TQ), F32)],
        compiler_params=pltpu.CompilerParams(dimension_semantics=("arbitrary", "arbitrary"),
                                             vmem_limit_bytes=VMEM_LIMIT),
        name="dsa_attention",
    )(qt, qit, wt, k, ki, vt, gz, yconv, x, mod3, gn_attn_col, w_out_bf, norm_f)


def _split_in_proj(w_in):
    conv = w_in[:, 0:2048]
    aq, ak, av, az = (w_in[:, 2048 + n * 512:2560 + n * 512] for n in range(4))
    iq = w_in[:, 4096:4608]
    ik = w_in[:, 4608:4672]
    iw = w_in[:, 4672:4680]
    w_nat = jnp.concatenate([conv, az], axis=1).astype(BF16)
    pad = jnp.zeros((D_MODEL, R_END - R_W - N_IDX_HEADS), w_in.dtype)
    w_cm = jnp.concatenate([aq, ak, av, iq, ik, ik, iw, pad], axis=1).T.astype(BF16)
    return w_nat, w_cm


def kernel(x, c, positions, w_ada, b_ada, norm_in, w_in, conv_w, conv_b, gn_conv, gn_attn, w_out, norm_f):
    bsz, seq, _ = x.shape
    mod3 = _ada_call(c, w_ada, b_ada).reshape(bsz, 1, 3 * D_MODEL)
    w_nat, w_cm = _split_in_proj(w_in)
    group = jnp.arange(D_CONV) // (D_CONV // CONV_GROUPS)
    gmat = jnp.where(group[:, None] == group[None, :], 1.0 / (D_CONV // CONV_GROUPS), 0.0).astype(BF16)
    yconv, gz, qt, k, vt, qit, ki, wt = _proj_call(
        x, positions.reshape(bsz, 1, seq), mod3, norm_in.reshape(1, D_MODEL), w_nat, w_cm,
        conv_w, conv_b.reshape(1, D_CONV), gn_conv.reshape(1, D_CONV), gmat)
    return _attn_call(qt, qit, wt, k, ki, vt, gz, yconv, x, mod3, gn_attn.reshape(D_ATTN, 1),
                      w_out.astype(BF16), norm_f.reshape(1, D_MODEL))
```

```python
import functools
import math

import jax
import jax.numpy as jnp
from jax import lax
from jax.experimental import pallas as pl
from jax.experimental.pallas import tpu as pltpu

D_MODEL = 1024
D_CONV = 512
D_ATTN = 512
HEAD_DIM = 64
N_HEADS = 8
CONV_GROUPS = 8
CONV_WIDTH = 3
ROPE_HALF = 8
ROPE_THETA = 500000.0
N_IDX_HEADS = 8
IDX_DIM = 64
TOPK = 256
EPS = 1e-6

LANES = 128
PAIR = 2 * HEAD_DIM
TS = 512
TQ = 512
TK = 512
HEADS_PER_PASS = 4
HALF = TQ // 2
assert TK == TQ and HALF >= TOPK
FINE_BITS = 17
NEG = -1e30
LOG2_E = math.log2(math.e)
F32_MAX = float(jnp.finfo(jnp.float32).max)
INT_MIN = -(2 ** 31)
KEY_LOWEST_FINITE = INT_MIN + 0x00800000
KEY_NEG_INF_BF16 = INT_MIN + 0x007F0000
VMEM_LIMIT = 56 * 1024 * 1024

F32 = jnp.float32
BF16 = jnp.bfloat16
I32 = jnp.int32


def _silu(v):
    return v * jax.nn.sigmoid(v)


def _ada_kernel(c_ref, w_ref, b_ref, o_ref):
    o_ref[...] = jnp.dot(_silu(c_ref[...]), w_ref[...], precision=lax.Precision.HIGHEST,
                         preferred_element_type=F32) + b_ref[...]


def _ada_call(c, w_ada, b_ada):
    bsz = c.shape[0]
    n_blk = 3
    return pl.pallas_call(
        _ada_kernel,
        grid=(n_blk,),
        in_specs=[pl.BlockSpec((bsz, D_MODEL), lambda n: (0, 0)),
                  pl.BlockSpec((D_MODEL, D_MODEL), lambda n: (0, n)),
                  pl.BlockSpec((1, D_MODEL), lambda n: (0, n))],
        out_specs=pl.BlockSpec((bsz, D_MODEL), lambda n: (0, n)),
        out_shape=jax.ShapeDtypeStruct((bsz, 3 * D_MODEL), F32),
        compiler_params=pltpu.CompilerParams(dimension_semantics=("arbitrary",)),
        name="adaln_mod",
    )(c, w_ada, b_ada.reshape(1, 3 * D_MODEL))


R_Q, R_K, R_V, R_QI, R_KI, R_W, R_END = 0, 512, 1024, 1536, 2048, 2176, 2192


def _rope_cm(xt, n_heads, cs, sn):
    t = xt.shape[-1]
    xr = xt.reshape(n_heads, HEAD_DIM, t)
    x1 = xr[:, 0:ROPE_HALF, :]
    x2 = xr[:, ROPE_HALF:2 * ROPE_HALF, :]
    r1 = x1 * cs - x2 * sn
    r2 = x1 * sn + x2 * cs
    return jnp.concatenate([r1, r2, xr[:, 2 * ROPE_HALF:, :]], axis=1).reshape(n_heads * HEAD_DIM, t)


def _pad_heads_to_pairs(xt):
    z = jnp.zeros((HEAD_DIM, xt.shape[-1]), xt.dtype)
    pieces = []
    for h in range(N_HEADS):
        blk = xt[h * HEAD_DIM:(h + 1) * HEAD_DIM, :]
        pieces += [blk, z] if h % 2 == 0 else [z, blk]
    return jnp.concatenate(pieces, axis=0)


def _proj_kernel(x_ref, pos_ref, mod_ref, normin_ref, wnat_ref, wcm_ref, convw_ref, convb_ref,
                 gnc_ref, gmat_ref,
                 yconv_ref, gz_ref, qt_ref, k_ref, vt_ref, qit_ref, ki_ref, wt_ref,
                 sbuf_ref):
    j = pl.program_id(1)
    x = x_ref[0]
    inv = lax.rsqrt(jnp.mean(x * x, axis=-1, keepdims=True) + EPS)
    mod = mod_ref[0]
    shift = mod[:, 0:D_MODEL]
    scale = mod[:, D_MODEL:2 * D_MODEL]
    h = (x * inv) * normin_ref[...] * (1.0 + scale) + shift
    hb = h.astype(BF16)

    nat = jnp.dot(hb, wnat_ref[...], preferred_element_type=F32)
    cb = nat[:, 0:512]
    cc = nat[:, 512:1024]
    cu = nat[:, 1024:1536]
    cz = nat[:, 1536:2048]
    az = nat[:, 2048:2560]
    gz_ref[0] = _silu(az)

    s = cc * cu

    @pl.when(j == 0)
    def _():
        sbuf_ref[0:8, :] = jnp.zeros((8, D_CONV), F32)

    sbuf_ref[8:TS + 8, :] = s
    s1 = sbuf_ref[7:TS + 7, :]
    s2 = sbuf_ref[6:TS + 6, :]
    cw = convw_ref[...]
    conv = convb_ref[...] + ((cw[0:1, :] * s2 + cw[1:2, :] * s1) + cw[2:3, :] * s)
    sbuf_ref[0:8, :] = sbuf_ref[TS:TS + 8, :]
    y = cb * conv
    ysq = y * y
    hi = ysq.astype(BF16)
    lo = (ysq - hi.astype(F32)).astype(BF16)
    gm = gmat_ref[...]
    ms = jnp.dot(hi, gm, preferred_element_type=F32) + jnp.dot(lo, gm, preferred_element_type=F32)
    yconv_ref[0] = ((y * lax.rsqrt(ms + EPS)) * gnc_ref[...] * _silu(cz)).astype(BF16)

    ht = h.T.astype(BF16)

    def proj_cm(r0, r1):
        return jnp.dot(wcm_ref[r0:r1, :], ht, preferred_element_type=F32)

    pos = pos_ref[0].astype(F32)
    jj = lax.broadcasted_iota(I32, (ROPE_HALF, 1), 0).astype(F32)
    theta = jnp.full((ROPE_HALF, 1), ROPE_THETA, F32)
    inv_freq = jnp.exp(jnp.log(theta) * (jj * (-1.0 / ROPE_HALF)))
    ang = inv_freq * pos
    cs = jnp.cos(ang)
    sn = jnp.sin(ang)

    scale_pow2 = HEAD_DIM ** -0.5
    q = _rope_cm(proj_cm(R_Q, R_K), N_HEADS, cs, sn) * (scale_pow2 * LOG2_E)
    qt_ref[0] = _pad_heads_to_pairs(q).astype(BF16)
    k = _rope_cm(proj_cm(R_K, R_V), N_HEADS, cs, sn)
    k_ref[0] = k.T.astype(BF16)
    vt_ref[0] = proj_cm(R_V, R_QI).astype(BF16)
    qi = _rope_cm(proj_cm(R_QI, R_KI), N_IDX_HEADS, cs, sn) * scale_pow2
    qit_ref[0] = _pad_heads_to_pairs(qi).astype(BF16)
    kiw = proj_cm(R_KI, R_END)
    ki2 = _rope_cm(kiw[0:PAIR, :], 2, cs, sn)
    ki_ref[0] = ki2.T.astype(BF16)
    wt_ref[0] = kiw[PAIR:PAIR + N_IDX_HEADS, :] * (N_IDX_HEADS ** -0.5)


def _proj_call(x, pos3, mod3, norm_in, w_nat, w_cm, conv_w, conv_b, gn_conv, gmat):
    bsz, seq, _ = x.shape
    const = lambda b, j: (0, 0)
    rows = lambda b, j: (b, j, 0)
    cols = lambda b, j: (b, 0, j)
    out_shapes = (
        jax.ShapeDtypeStruct((bsz, seq, D_CONV), BF16),
        jax.ShapeDtypeStruct((bsz, seq, D_ATTN), F32),
        jax.ShapeDtypeStruct((bsz, N_HEADS * PAIR, seq), BF16),
        jax.ShapeDtypeStruct((bsz, seq, D_ATTN), BF16),
        jax.ShapeDtypeStruct((bsz, D_ATTN, seq), BF16),
        jax.ShapeDtypeStruct((bsz, N_IDX_HEADS * PAIR, seq), BF16),
        jax.ShapeDtypeStruct((bsz, seq, PAIR), BF16),
        jax.ShapeDtypeStruct((bsz, N_IDX_HEADS, seq), F32),
    )
    return pl.pallas_call(
        _proj_kernel,
        grid=(bsz, seq // TS),
        in_specs=[
            pl.BlockSpec((1, TS, D_MODEL), rows),
            pl.BlockSpec((1, 1, TS), cols),
            pl.BlockSpec((1, 1, 3 * D_MODEL), lambda b, j: (b, 0, 0)),
            pl.BlockSpec((1, D_MODEL), const),
            pl.BlockSpec(w_nat.shape, const),
            pl.BlockSpec(w_cm.shape, const),
            pl.BlockSpec((CONV_WIDTH, D_CONV), const),
            pl.BlockSpec((1, D_CONV), const),
            pl.BlockSpec((1, D_CONV), const),
            pl.BlockSpec((D_CONV, D_CONV), const),
        ],
        out_specs=(
            pl.BlockSpec((1, TS, D_CONV), rows),
            pl.BlockSpec((1, TS, D_ATTN), rows),
            pl.BlockSpec((1, N_HEADS * PAIR, TS), cols),
            pl.BlockSpec((1, TS, D_ATTN), rows),
            pl.BlockSpec((1, D_ATTN, TS), cols),
            pl.BlockSpec((1, N_IDX_HEADS * PAIR, TS), cols),
            pl.BlockSpec((1, TS, PAIR), rows),
            pl.BlockSpec((1, N_IDX_HEADS, TS), cols),
        ),
        out_shape=out_shapes,
        scratch_shapes=[pltpu.VMEM((TS + 8, D_CONV), F32)],
        compiler_params=pltpu.CompilerParams(dimension_semantics=("arbitrary", "arbitrary"),
                                             vmem_limit_bytes=VMEM_LIMIT),
        name="in_proj",
    )(x, pos3, mod3, norm_in, w_nat, w_cm, conv_w, conv_b, gn_conv, gmat)


def _key_to_f32(key):
    bits = jnp.where(key < 0, key ^ 0x7FFFFFFF, key)
    return lax.bitcast_convert_type(bits, F32)


def _attn_kernel(qt_ref, qit_ref, wt_ref, k_ref, ki_ref, vt_ref, gz_ref, yconv_ref, x_ref, mod_ref,
                 gna_ref, wout_ref, normf_ref, o_ref, sc_ref, sb_ref, y_ref, s8_ref, thr_ref):
    i = pl.program_id(1)
    n_chunks = i + 1
    row = lax.broadcasted_iota(I32, (TK, TQ), 0)
    col = lax.broadcasted_iota(I32, (TK, TQ), 1)

    def chunk_off(c):
        return pl.multiple_of(c * TK, TK)

    def score_tile(key0, n_keys, lane0, n_lanes):
        lanes = slice(lane0, lane0 + n_lanes)
        kc = ki_ref[0, pl.ds(key0, n_keys), :]
        acc = jnp.zeros((n_keys, n_lanes), F32)
        for hh in range(N_IDX_HEADS):
            d = jnp.dot(kc, qit_ref[0, hh * PAIR:(hh + 1) * PAIR, lanes], preferred_element_type=F32)
            acc = acc + wt_ref[0, hh:hh + 1, lanes] * jnp.maximum(d, 0.0)
        return acc

    def score_chunk(c):
        return score_tile(chunk_off(c), TK, 0, TQ)

    def score_diagonal():
        d0 = chunk_off(i)
        top = jnp.where(lax.broadcasted_iota(I32, (HALF, TQ), 0) <= lax.broadcasted_iota(I32, (HALF, TQ), 1),
                        score_tile(d0, HALF, 0, TQ), -jnp.inf)
        corner = jnp.where(
            lax.broadcasted_iota(I32, (HALF, HALF), 0) <= lax.broadcasted_iota(I32, (HALF, HALF), 1),
            score_tile(pl.multiple_of(d0 + HALF, HALF), HALF, HALF, HALF), -jnp.inf)
        bottom = jnp.concatenate([jnp.full((HALF, HALF), -jnp.inf, F32), corner], axis=1)
        return jnp.concatenate([top, bottom], axis=0)

    def put_scores(c, tile):
        sc_ref[pl.ds(chunk_off(c), TK), :] = tile
        sb_ref[pl.ds(chunk_off(c), TK), :] = tile.astype(BF16)

    def score_body(c, carry):
        put_scores(c, score_chunk(c))
        return carry

    lax.fori_loop(0, i, score_body, 0)
    put_scores(i, score_diagonal())

    def count_chunks(pred):
        def body(c, acc):
            tile = sc_ref[pl.ds(chunk_off(c), TK), :]
            return acc + jnp.sum(pred(tile, c).reshape(TK // 8, 8, TQ), axis=0)
        acc = lax.fori_loop(0, n_chunks, body, jnp.zeros((8, TQ), I32))
        return jnp.sum(acc, axis=0, keepdims=True)

    def swept_counts(src_ref, rows_per_tile, tile_hits):
        def body(c, acc):
            return acc + tile_hits(src_ref[pl.ds(chunk_off(c), TK), :], slice(None))
        acc = lax.fori_loop(0, i, body, jnp.zeros((rows_per_tile, TQ), F32))
        d0 = chunk_off(i)
        acc = acc + tile_hits(src_ref[pl.ds(d0, HALF), :], slice(None))
        right = tile_hits(src_ref[pl.ds(pl.multiple_of(d0 + HALF, HALF), HALF), HALF:],
                          slice(HALF, None))
        return acc + jnp.concatenate([jnp.zeros((rows_per_tile, HALF), F32), right], axis=1)

    def count_ge(thr):
        def tile_hits(tile, lanes):
            hit = jnp.where(tile >= thr[:, lanes], 1.0, 0.0)
            return jnp.sum(hit.reshape(tile.shape[0] // 8, 8, tile.shape[1]), axis=0)
        return jnp.sum(swept_counts(sc_ref, 8, tile_hits), axis=0, keepdims=True).astype(I32)

    def count_ge_bf16(thr_b):
        def tile_hits(tile, lanes):
            hit = jnp.where(tile >= thr_b[:, lanes], jnp.ones((), BF16), jnp.zeros((), BF16))
            hit = hit.reshape(tile.shape[0] // 16, 16, tile.shape[1])
            part = hit[0]
            for r in range(1, tile.shape[0] // 16):
                part = part + hit[r]
            return part.astype(F32)
        return jnp.sum(swept_counts(sb_ref, 16, tile_hits), axis=0, keepdims=True).astype(I32)

    def coarse_body(b, lo_u):
        cand_u = lo_u | lax.shift_left(jnp.int32(1), 31 - b)
        key = cand_u ^ INT_MIN
        bits = jnp.where(key < 0, key ^ 0x7FFF0000, key)
        cand = lax.bitcast_convert_type(bits, F32).astype(BF16)
        cnt = jnp.where(key < KEY_NEG_INF_BF16, TOPK, count_ge_bf16(cand))
        return jnp.where(cnt >= TOPK, cand_u, lo_u)

    coarse_u = lax.fori_loop(0, 16, coarse_body, jnp.zeros((1, TQ), I32))
    coarse_key = coarse_u ^ INT_MIN
    coarse_key = jnp.where(coarse_key < 0, coarse_key | 0xFFFF, coarse_key)
    base_key = coarse_key - (1 << (FINE_BITS - 1))

    def fine_body(b, carry):
        off, cnt_lo = carry
        cand_off = off | lax.shift_left(jnp.int32(1), FINE_BITS - 1 - b)
        cnt = count_ge(_key_to_f32(base_key + cand_off))
        ok = cnt >= TOPK
        return jnp.where(ok, cand_off, off), jnp.where(ok, cnt, cnt_lo)

    off, cnt_lo = lax.fori_loop(0, FINE_BITS, fine_body,
                                (jnp.zeros((1, TQ), I32), jnp.zeros((1, TQ), I32)))
    lo = base_key + off
    has_kth = lo >= KEY_LOWEST_FINITE
    thr0 = jnp.where(has_kth, _key_to_f32(lo), -F32_MAX)
    tie0 = jnp.where(has_kth, cnt_lo, 0) > TOPK

    thr_ref[...] = thr0

    @pl.when(jnp.max(tie0.astype(I32)) > 0)
    def _():
        def midpoint(lo_f, hi_f):
            return lo_f + (hi_f - lo_f) * 0.5

        def splits(lo_f, hi_f):
            mid = midpoint(lo_f, hi_f)
            return jnp.where(mid > lo_f, jnp.where(mid < hi_f, 1, 0), 0).astype(I32)

        def refine_cond(carry):
            lo_f, hi_f, _, it = carry
            return jnp.logical_and(jnp.max(splits(lo_f, hi_f)) > 0, it < 64)

        def refine_body(carry):
            lo_f, hi_f, cnt_f, it = carry
            mid = midpoint(lo_f, hi_f)
            cnt = count_ge(mid)
            up = jnp.where(splits(lo_f, hi_f) > 0, jnp.where(cnt >= TOPK, 1, 0), 0) > 0
            down = jnp.where(splits(lo_f, hi_f) > 0, jnp.where(cnt >= TOPK, 0, 1), 0) > 0
            return (jnp.where(up, mid, lo_f), jnp.where(down, mid, hi_f),
                    jnp.where(up, cnt, cnt_f), it + 1)

        above0 = jnp.where(tie0, _key_to_f32(lo + 1), thr0)
        thr, _, cnt_thr, _ = lax.while_loop(refine_cond, refine_body,
                                            (thr0, above0, cnt_lo, jnp.int32(0)))
        thr_ref[...] = thr

        tie = jnp.where(tie0, cnt_thr, 0) > TOPK
        need = TOPK - count_chunks(lambda tile, c: (tile > thr).astype(I32))

        keep = jnp.where(tie, need, jnp.int32(2 ** 30)).astype(F32)
        prefix = jnp.where(row >= col, 1.0, 0.0).astype(BF16)

        def drop_body(c, seen):
            tile = sc_ref[pl.ds(chunk_off(c), TK), :]
            tied = tile == thr
            rank = seen + jnp.dot(prefix, jnp.where(tied, 1.0, 0.0).astype(BF16),
                                  preferred_element_type=F32)
            dropped = jnp.where(rank > keep, -jnp.inf, tile)
            sc_ref[pl.ds(chunk_off(c), TK), :] = jnp.where(tied, dropped, tile)
            return rank[TK - 1:TK, :]

        lax.fori_loop(0, n_chunks, drop_body, jnp.zeros((1, TQ), F32))

    thr = thr_ref[...]

    def bias_body(c, carry):
        tile = sc_ref[pl.ds(chunk_off(c), TK), :]
        sc_ref[pl.ds(chunk_off(c), TK), :] = jnp.where(tile >= thr, 0.0, NEG)
        return carry

    lax.fori_loop(0, n_chunks, bias_body, 0)

    y_ref[...] = jnp.zeros((D_ATTN, TQ), F32)

    def att_body(c, carry, heads):
        ms, ls = carry
        off = chunk_off(c)
        new_ms, new_ls = [], []
        bias = sc_ref[pl.ds(off, TK), :]
        for n, h in enumerate(heads):
            lane0 = (h // 2) * PAIR
            q2 = qt_ref[0, h * PAIR:(h + 1) * PAIR, :]
            k2 = k_ref[0, pl.ds(off, TK), lane0:lane0 + PAIR]
            s8_ref[n] = jnp.dot(k2, q2, preferred_element_type=F32) + bias
        for n, h in enumerate(heads):
            rows_h = slice(h * HEAD_DIM, (h + 1) * HEAD_DIM)
            s = s8_ref[n]
            m_new = jnp.maximum(ms[n], jnp.max(s, axis=0, keepdims=True))
            p = jnp.exp2(s - m_new)
            alpha = jnp.exp2(ms[n] - m_new)
            new_ms.append(m_new)
            new_ls.append(alpha * ls[n] + jnp.sum(p, axis=0, keepdims=True))
            vc = vt_ref[0, rows_h, pl.ds(off, TK)]
            y_ref[rows_h, :] = alpha * y_ref[rows_h, :] + jnp.dot(
                vc, p.astype(BF16), preferred_element_type=F32)
        return tuple(new_ms), tuple(new_ls)

    ls = []
    for g in range(N_HEADS // HEADS_PER_PASS):
        heads = tuple(range(g * HEADS_PER_PASS, (g + 1) * HEADS_PER_PASS))
        init = (tuple(jnp.full((1, TQ), NEG, F32) for _ in heads),
                tuple(jnp.zeros((1, TQ), F32) for _ in heads))
        _, ls_g = lax.fori_loop(0, n_chunks, functools.partial(att_body, heads=heads), init)
        ls += list(ls_g)
    for h in range(N_HEADS):
        rows_h = slice(h * HEAD_DIM, (h + 1) * HEAD_DIM)
        out = y_ref[rows_h, :] / ls[h]
        ms_h = jnp.mean(out * out, axis=0, keepdims=True)
        y_ref[rows_h, :] = (out * lax.rsqrt(ms_h + EPS)) * gna_ref[rows_h, :]

    ya = (y_ref[...].T * gz_ref[0]).astype(BF16)
    yo = (jnp.dot(yconv_ref[0], wout_ref[0:D_CONV, :], preferred_element_type=F32)
          + jnp.dot(ya, wout_ref[D_CONV:D_CONV + D_ATTN, :], preferred_element_type=F32))
    gate = mod_ref[0][:, 2 * D_MODEL:3 * D_MODEL]
    o = x_ref[0] + gate * yo
    inv = lax.rsqrt(jnp.mean(o * o, axis=-1, keepdims=True) + EPS)
    o_ref[0] = (o * inv) * normf_ref[...]


def _attn_call(qt, qit, wt, k, ki, vt, gz, yconv, x, mod3, gn_attn_col, w_out_bf, norm_f):
    bsz, seq, _ = x.shape
    const = lambda b, i: (0, 0)
    rows = lambda b, i: (b, i, 0)
    cols = lambda b, i: (b, 0, i)
    whole = lambda b, i: (b, 0, 0)
    once = pl.Buffered(1)
    return pl.pallas_call(
        _attn_kernel,
        grid=(bsz, seq // TQ),
        in_specs=[
            pl.BlockSpec((1, N_HEADS * PAIR, TQ), cols),
            pl.BlockSpec((1, N_IDX_HEADS * PAIR, TQ), cols),
            pl.BlockSpec((1, N_IDX_HEADS, TQ), cols),
            pl.BlockSpec((1, seq, D_ATTN), whole, pipeline_mode=once),
            pl.BlockSpec((1, seq, PAIR), whole, pipeline_mode=once),
            pl.BlockSpec((1, D_ATTN, seq), whole, pipeline_mode=once),
            pl.BlockSpec((1, TQ, D_ATTN), rows),
            pl.BlockSpec((1, TQ, D_CONV), rows),
            pl.BlockSpec((1, TQ, D_MODEL), rows),
            pl.BlockSpec((1, 1, 3 * D_MODEL), whole),
            pl.BlockSpec((D_ATTN, 1), const),
            pl.BlockSpec((D_CONV + D_ATTN, D_MODEL), const, pipeline_mode=once),
            pl.BlockSpec((1, D_MODEL), const),
        ],
        out_specs=pl.BlockSpec((1, TQ, D_MODEL), rows),
        out_shape=jax.ShapeDtypeStruct((bsz, seq, D_MODEL), F32),
        scratch_shapes=[pltpu.VMEM((seq, TQ), F32), pltpu.VMEM((seq, TQ), BF16),
                        pltpu.VMEM((D_ATTN, TQ), F32),
                        pltpu.VMEM((HEADS_PER_PASS, TK, TQ), F32), pltpu.VMEM((1, TQ), F32)],
        compiler_params=pltpu.CompilerParams(dimension_semantics=("arbitrary", "arbitrary"),
                                             vmem_limit_bytes=VMEM_LIMIT),
        name="dsa_attention",
    )(qt, qit, wt, k, ki, vt, gz, yconv, x, mod3, gn_attn_col, w_out_bf, norm_f)


def _split_in_proj(w_in):
    conv = w_in[:, 0:2048]
    aq, ak, av, az = (w_in[:, 2048 + n * 512:2560 + n * 512] for n in range(4))
    iq = w_in[:, 4096:4608]
    ik = w_in[:, 4608:4672]
    iw = w_in[:, 4672:4680]
    w_nat = jnp.concatenate([conv, az], axis=1).astype(BF16)
    pad = jnp.zeros((D_MODEL, R_END - R_W - N_IDX_HEADS), w_in.dtype)
    w_cm = jnp.concatenate([aq, ak, av, iq, ik, ik, iw, pad], axis=1).T.astype(BF16)
    return w_nat, w_cm


def kernel(x, c, positions, w_ada, b_ada, norm_in, w_in, conv_w, conv_b, gn_conv, gn_attn, w_out, norm_f):
    bsz, seq, _ = x.shape
    mod3 = _ada_call(c, w_ada, b_ada).reshape(bsz, 1, 3 * D_MODEL)
    w_nat, w_cm = _split_in_proj(w_in)
    group = jnp.arange(D_CONV) // (D_CONV // CONV_GROUPS)
    gmat = jnp.where(group[:, None] == group[None, :], 1.0 / (D_CONV // CONV_GROUPS), 0.0).astype(BF16)
    yconv, gz, qt, k, vt, qit, ki, wt = _proj_call(
        x, positions.reshape(bsz, 1, seq), mod3, norm_in.reshape(1, D_MODEL), w_nat, w_cm,
        conv_w, conv_b.reshape(1, D_CONV), gn_conv.reshape(1, D_CONV), gmat)
    return _attn_call(qt, qit, wt, k, ki, vt, gz, yconv, x, mod3, gn_attn.reshape(D_ATTN, 1),
                      w_out.astype(BF16), norm_f.reshape(1, D_MODEL))
```

```python
import functools
import math

import jax
import jax.numpy as jnp
from jax import lax
from jax.experimental import pallas as pl
from jax.experimental.pallas import tpu as pltpu

D_MODEL = 1024
D_CONV = 512
D_ATTN = 512
HEAD_DIM = 64
N_HEADS = 8
CONV_GROUPS = 8
CONV_WIDTH = 3
ROPE_HALF = 8
ROPE_THETA = 500000.0
N_IDX_HEADS = 8
IDX_DIM = 64
TOPK = 256
EPS = 1e-6

LANES = 128
PAIR = 2 * HEAD_DIM
TS = 512
TQ = 512
TK = 512
HEADS_PER_PASS = 4
HALF = TQ // 2
assert TK == TQ and HALF >= TOPK
FINE_BITS = 17
NEG = -1e30
LOG2_E = math.log2(math.e)
F32_MAX = float(jnp.finfo(jnp.float32).max)
INT_MIN = -(2 ** 31)
KEY_LOWEST_FINITE = INT_MIN + 0x00800000
KEY_NEG_INF_BF16 = INT_MIN + 0x007F0000
VMEM_LIMIT = 56 * 1024 * 1024

F32 = jnp.float32
BF16 = jnp.bfloat16
I32 = jnp.int32


def _silu(v):
    return v * jax.nn.sigmoid(v)


def _ada_kernel(c_ref, w_ref, b_ref, o_ref):
    o_ref[...] = jnp.dot(_silu(c_ref[...]), w_ref[...], precision=lax.Precision.HIGHEST,
                         preferred_element_type=F32) + b_ref[...]


def _ada_call(c, w_ada, b_ada):
    bsz = c.shape[0]
    n_blk = 3
    return pl.pallas_call(
        _ada_kernel,
        grid=(n_blk,),
        in_specs=[pl.BlockSpec((bsz, D_MODEL), lambda n: (0, 0)),
                  pl.BlockSpec((D_MODEL, D_MODEL), lambda n: (0, n)),
                  pl.BlockSpec((1, D_MODEL), lambda n: (0, n))],
        out_specs=pl.BlockSpec((bsz, D_MODEL), lambda n: (0, n)),
        out_shape=jax.ShapeDtypeStruct((bsz, 3 * D_MODEL), F32),
        compiler_params=pltpu.CompilerParams(dimension_semantics=("arbitrary",)),
        name="adaln_mod",
    )(c, w_ada, b_ada.reshape(1, 3 * D_MODEL))


R_Q, R_K, R_V, R_QI, R_KI, R_W, R_END = 0, 512, 1024, 1536, 2048, 2176, 2192


def _rope_cm(xt, n_heads, cs, sn):
    t = xt.shape[-1]
    xr = xt.reshape(n_heads, HEAD_DIM, t)
    x1 = xr[:, 0:ROPE_HALF, :]
    x2 = xr[:, ROPE_HALF:2 * ROPE_HALF, :]
    r1 = x1 * cs - x2 * sn
    r2 = x1 * sn + x2 * cs
    return jnp.concatenate([r1, r2, xr[:, 2 * ROPE_HALF:, :]], axis=1).reshape(n_heads * HEAD_DIM, t)


def _pad_heads_to_pairs(xt):
    z = jnp.zeros((HEAD_DIM, xt.shape[-1]), xt.dtype)
    pieces = []
    for h in range(N_HEADS):
        blk = xt[h * HEAD_DIM:(h + 1) * HEAD_DIM, :]
        pieces += [blk, z] if h % 2 == 0 else [z, blk]
    return jnp.concatenate(pieces, axis=0)


def _proj_kernel(x_ref, pos_ref, mod_ref, normin_ref, wnat_ref, wcm_ref, convw_ref, convb_ref,
                 gnc_ref, gmat_ref,
                 yconv_ref, gz_ref, qt_ref, k_ref, vt_ref, qit_ref, ki_ref, wt_ref,
                 sbuf_ref):
    j = pl.program_id(1)
    x = x_ref[0]
    inv = lax.rsqrt(jnp.mean(x * x, axis=-1, keepdims=True) + EPS)
    mod = mod_ref[0]
    shift = mod[:, 0:D_MODEL]
    scale = mod[:, D_MODEL:2 * D_MODEL]
    h = (x * inv) * normin_ref[...] * (1.0 + scale) + shift
    hb = h.astype(BF16)

    nat = jnp.dot(hb, wnat_ref[...], preferred_element_type=F32)
    cb = nat[:, 0:512]
    cc = nat[:, 512:1024]
    cu = nat[:, 1024:1536]
    cz = nat[:, 1536:2048]
    az = nat[:, 2048:2560]
    gz_ref[0] = _silu(az)

    s = cc * cu

    @pl.when(j == 0)
    def _():
        sbuf_ref[0:8, :] = jnp.zeros((8, D_CONV), F32)

    sbuf_ref[8:TS + 8, :] = s
    s1 = sbuf_ref[7:TS + 7, :]
    s2 = sbuf_ref[6:TS + 6, :]
    cw = convw_ref[...]
    conv = convb_ref[...] + ((cw[0:1, :] * s2 + cw[1:2, :] * s1) + cw[2:3, :] * s)
    sbuf_ref[0:8, :] = sbuf_ref[TS:TS + 8, :]
    y = cb * conv
    ysq = y * y
    hi = ysq.astype(BF16)
    lo = (ysq - hi.astype(F32)).astype(BF16)
    gm = gmat_ref[...]
    ms = jnp.dot(hi, gm, preferred_element_type=F32) + jnp.dot(lo, gm, preferred_element_type=F32)
    yconv_ref[0] = ((y * lax.rsqrt(ms + EPS)) * gnc_ref[...] * _silu(cz)).astype(BF16)

    ht = h.T.astype(BF16)

    def proj_cm(r0, r1):
        return jnp.dot(wcm_ref[r0:r1, :], ht, preferred_element_type=F32)

    pos = pos_ref[0].astype(F32)
    jj = lax.broadcasted_iota(I32, (ROPE_HALF, 1), 0).astype(F32)
    theta = jnp.full((ROPE_HALF, 1), ROPE_THETA, F32)
    inv_freq = jnp.exp(jnp.log(theta) * (jj * (-1.0 / ROPE_HALF)))
    ang = inv_freq * pos
    cs = jnp.cos(ang)
    sn = jnp.sin(ang)

    scale_pow2 = HEAD_DIM ** -0.5
    q = _rope_cm(proj_cm(R_Q, R_K), N_HEADS, cs, sn) * (scale_pow2 * LOG2_E)
    qt_ref[0] = _pad_heads_to_pairs(q).astype(BF16)
    k = _rope_cm(proj_cm(R_K, R_V), N_HEADS, cs, sn)
    k_ref[0] = k.T.astype(BF16)
    vt_ref[0] = proj_cm(R_V, R_QI).astype(BF16)
    qi = _rope_cm(proj_cm(R_QI, R_KI), N_IDX_HEADS, cs, sn) * scale_pow2
    qit_ref[0] = _pad_heads_to_pairs(qi).astype(BF16)
    kiw = proj_cm(R_KI, R_END)
    ki2 = _rope_cm(kiw[0:PAIR, :], 2, cs, sn)
    ki_ref[0] = ki2.T.astype(BF16)
    wt_ref[0] = kiw[PAIR:PAIR + N_IDX_HEADS, :] * (N_IDX_HEADS ** -0.5)


def _proj_call(x, pos3, mod3, norm_in, w_nat, w_cm, conv_w, conv_b, gn_conv, gmat):
    bsz, seq, _ = x.shape
    const = lambda b, j: (0, 0)
    rows = lambda b, j: (b, j, 0)
    cols = lambda b, j: (b, 0, j)
    out_shapes = (
        jax.ShapeDtypeStruct((bsz, seq, D_CONV), BF16),
        jax.ShapeDtypeStruct((bsz, seq, D_ATTN), F32),
        jax.ShapeDtypeStruct((bsz, N_HEADS * PAIR, seq), BF16),
        jax.ShapeDtypeStruct((bsz, seq, D_ATTN), BF16),
        jax.ShapeDtypeStruct((bsz, D_ATTN, seq), BF16),
        jax.ShapeDtypeStruct((bsz, N_IDX_HEADS * PAIR, seq), BF16),
        jax.ShapeDtypeStruct((bsz, seq, PAIR), BF16),
        jax.ShapeDtypeStruct((bsz, N_IDX_HEADS, seq), F32),
    )
    return pl.pallas_call(
        _proj_kernel,
        grid=(bsz, seq // TS),
        in_specs=[
            pl.BlockSpec((1, TS, D_MODEL), rows),
            pl.BlockSpec((1, 1, TS), cols),
            pl.BlockSpec((1, 1, 3 * D_MODEL), lambda b, j: (b, 0, 0)),
            pl.BlockSpec((1, D_MODEL), const),
            pl.BlockSpec(w_nat.shape, const),
            pl.BlockSpec(w_cm.shape, const),
            pl.BlockSpec((CONV_WIDTH, D_CONV), const),
            pl.BlockSpec((1, D_CONV), const),
            pl.BlockSpec((1, D_CONV), const),
            pl.BlockSpec((D_CONV, D_CONV), const),
        ],
        out_specs=(
            pl.BlockSpec((1, TS, D_CONV), rows),
            pl.BlockSpec((1, TS, D_ATTN), rows),
            pl.BlockSpec((1, N_HEADS * PAIR, TS), cols),
            pl.BlockSpec((1, TS, D_ATTN), rows),
            pl.BlockSpec((1, D_ATTN, TS), cols),
            pl.BlockSpec((1, N_IDX_HEADS * PAIR, TS), cols),
            pl.BlockSpec((1, TS, PAIR), rows),
            pl.BlockSpec((1, N_IDX_HEADS, TS), cols),
        ),
        out_shape=out_shapes,
        scratch_shapes=[pltpu.VMEM((TS + 8, D_CONV), F32)],
        compiler_params=pltpu.CompilerParams(dimension_semantics=("arbitrary", "arbitrary"),
                                             vmem_limit_bytes=VMEM_LIMIT),
        name="in_proj",
    )(x, pos3, mod3, norm_in, w_nat, w_cm, conv_w, conv_b, gn_conv, gmat)


def _key_to_f32(key):
    bits = jnp.where(key < 0, key ^ 0x7FFFFFFF, key)
    return lax.bitcast_convert_type(bits, F32)


def _attn_kernel(qt_ref, qit_ref, wt_ref, k_ref, ki_ref, vt_ref, gz_ref, yconv_ref, x_ref, mod_ref,
                 gna_ref, wout_ref, normf_ref, o_ref, sc_ref, sb_ref, y_ref, s8_ref, thr_ref):
    i = pl.program_id(1)
    n_chunks = i + 1
    row = lax.broadcasted_iota(I32, (TK, TQ), 0)
    col = lax.broadcasted_iota(I32, (TK, TQ), 1)

    def chunk_off(c):
        return pl.multiple_of(c * TK, TK)

    def score_tile(key0, n_keys, lane0, n_lanes):
        lanes = slice(lane0, lane0 + n_lanes)
        kc = ki_ref[0, pl.ds(key0, n_keys), :]
        acc = jnp.zeros((n_keys, n_lanes), F32)
        for hh in range(N_IDX_HEADS):
            d = jnp.dot(kc, qit_ref[0, hh * PAIR:(hh + 1) * PAIR, lanes], preferred_element_type=F32)
            acc = acc + wt_ref[0, hh:hh + 1, lanes] * jnp.maximum(d, 0.0)
        return acc

    def score_chunk(c):
        return score_tile(chunk_off(c), TK, 0, TQ)

    def score_diagonal():
        d0 = chunk_off(i)
        top = jnp.where(lax.broadcasted_iota(I32, (HALF, TQ), 0) <= lax.broadcasted_iota(I32, (HALF, TQ), 1),
                        score_tile(d0, HALF, 0, TQ), -jnp.inf)
        corner = jnp.where(
            lax.broadcasted_iota(I32, (HALF, HALF), 0) <= lax.broadcasted_iota(I32, (HALF, HALF), 1),
            score_tile(pl.multiple_of(d0 + HALF, HALF), HALF, HALF, HALF), -jnp.inf)
        bottom = jnp.concatenate([jnp.full((HALF, HALF), -jnp.inf, F32), corner], axis=1)
        return jnp.concatenate([top, bottom], axis=0)

    def put_scores(c, tile):
        sc_ref[pl.ds(chunk_off(c), TK), :] = tile
        sb_ref[pl.ds(chunk_off(c), TK), :] = tile.astype(BF16)

    def score_body(c, carry):
        put_scores(c, score_chunk(c))
        return carry

    lax.fori_loop(0, i, score_body, 0)
    put_scores(i, score_diagonal())

    def count_chunks(pred):
        def body(c, acc):
            tile = sc_ref[pl.ds(chunk_off(c), TK), :]
            return acc + jnp.sum(pred(tile, c).reshape(TK // 8, 8, TQ), axis=0)
        acc = lax.fori_loop(0, n_chunks, body, jnp.zeros((8, TQ), I32))
        return jnp.sum(acc, axis=0, keepdims=True)

    def swept_counts(src_ref, rows_per_tile, tile_hits):
        def body(c, acc):
            return acc + tile_hits(src_ref[pl.ds(chunk_off(c), TK), :], slice(None))
        acc = lax.fori_loop(0, i, body, jnp.zeros((rows_per_tile, TQ), F32))
        d0 = chunk_off(i)
        acc = acc + tile_hits(src_ref[pl.ds(d0, HALF), :], slice(None))
        right = tile_hits(src_ref[pl.ds(pl.multiple_of(d0 + HALF, HALF), HALF), HALF:],
                          slice(HALF, None))
        return acc + jnp.concatenate([jnp.zeros((rows_per_tile, HALF), F32), right], axis=1)

    def count_ge(thr):
        def tile_hits(tile, lanes):
            hit = jnp.where(tile >= thr[:, lanes], 1.0, 0.0)
            return jnp.sum(hit.reshape(tile.shape[0] // 8, 8, tile.shape[1]), axis=0)
        return jnp.sum(swept_counts(sc_ref, 8, tile_hits), axis=0, keepdims=True).astype(I32)

    def count_ge_bf16(thr_b):
        def tile_hits(tile, lanes):
            hit = jnp.where(tile >= thr_b[:, lanes], jnp.ones((), BF16), jnp.zeros((), BF16))
            hit = hit.reshape(tile.shape[0] // 16, 16, tile.shape[1])
            part = hit[0]
            for r in range(1, tile.shape[0] // 16):
                part = part + hit[r]
            return part.astype(F32)
        return jnp.sum(swept_counts(sb_ref, 16, tile_hits), axis=0, keepdims=True).astype(I32)

    def coarse_body(b, lo_u):
        cand_u = lo_u | lax.shift_left(jnp.int32(1), 31 - b)
        key = cand_u ^ INT_MIN
        bits = jnp.where(key < 0, key ^ 0x7FFF0000, key)
        cand = lax.bitcast_convert_type(bits, F32).astype(BF16)
        cnt = jnp.where(key < KEY_NEG_INF_BF16, TOPK, count_ge_bf16(cand))
        return jnp.where(cnt >= TOPK, cand_u, lo_u)

    coarse_u = lax.fori_loop(0, 16, coarse_body, jnp.zeros((1, TQ), I32))
    coarse_key = coarse_u ^ INT_MIN
    coarse_key = jnp.where(coarse_key < 0, coarse_key | 0xFFFF, coarse_key)
    base_key = coarse_key - (1 << (FINE_BITS - 1))

    def fine_body(b, carry):
        off, cnt_lo = carry
        cand_off = off | lax.shift_left(jnp.int32(1), FINE_BITS - 1 - b)
        cnt = count_ge(_key_to_f32(base_key + cand_off))
        ok = cnt >= TOPK
        return jnp.where(ok, cand_off, off), jnp.where(ok, cnt, cnt_lo)

    off, cnt_lo = lax.fori_loop(0, FINE_BITS, fine_body,
                                (jnp.zeros((1, TQ), I32), jnp.zeros((1, TQ), I32)))
    lo = base_key + off
    has_kth = lo >= KEY_LOWEST_FINITE
    thr0 = jnp.where(has_kth, _key_to_f32(lo), -F32_MAX)
    tie0 = jnp.where(has_kth, cnt_lo, 0) > TOPK

    thr_ref[...] = thr0

    @pl.when(jnp.max(tie0.astype(I32)) > 0)
    def _():
        def midpoint(lo_f, hi_f):
            return lo_f + (hi_f - lo_f) * 0.5

        def splits(lo_f, hi_f):
            mid = midpoint(lo_f, hi_f)
            return jnp.where(mid > lo_f, jnp.where(mid < hi_f, 1, 0), 0).astype(I32)

        def refine_cond(carry):
            lo_f, hi_f, _, it = carry
            return jnp.logical_and(jnp.max(splits(lo_f, hi_f)) > 0, it < 64)

        def refine_body(carry):
            lo_f, hi_f, cnt_f, it = carry
            mid = midpoint(lo_f, hi_f)
            cnt = count_ge(mid)
            up = jnp.where(splits(lo_f, hi_f) > 0, jnp.where(cnt >= TOPK, 1, 0), 0) > 0
            down = jnp.where(splits(lo_f, hi_f) > 0, jnp.where(cnt >= TOPK, 0, 1), 0) > 0
            return (jnp.where(up, mid, lo_f), jnp.where(down, mid, hi_f),
                    jnp.where(up, cnt, cnt_f), it + 1)

        above0 = jnp.where(tie0, _key_to_f32(lo + 1), thr0)
        thr, _, cnt_thr, _ = lax.while_loop(refine_cond, refine_body,
                                            (thr0, above0, cnt_lo, jnp.int32(0)))
        thr_ref[...] = thr

        tie = jnp.where(tie0, cnt_thr, 0) > TOPK
        need = TOPK - count_chunks(lambda tile, c: (tile > thr).astype(I32))

        keep = jnp.where(tie, need, jnp.int32(2 ** 30)).astype(F32)
        prefix = jnp.where(row >= col, 1.0, 0.0).astype(BF16)

        def drop_body(c, seen):
            tile = sc_ref[pl.ds(chunk_off(c), TK), :]
            tied = tile == thr
            rank = seen + jnp.dot(prefix, jnp.where(tied, 1.0, 0.0).astype(BF16),
                                  preferred_element_type=F32)
            dropped = jnp.where(rank > keep, -jnp.inf, tile)
            sc_ref[pl.ds(chunk_off(c), TK), :] = jnp.where(tied, dropped, tile)
            return rank[TK - 1:TK, :]

        lax.fori_loop(0, n_chunks, drop_body, jnp.zeros((1, TQ), F32))

    thr = thr_ref[...]

    def bias_body(c, carry):
        tile = sc_ref[pl.ds(chunk_off(c), TK), :]
        sc_ref[pl.ds(chunk_off(c), TK), :] = jnp.where(tile >= thr, 0.0, NEG)
        return carry

    lax.fori_loop(0, n_chunks, bias_body, 0)

    y_ref[...] = jnp.zeros((D_ATTN, TQ), F32)

    def att_body(c, carry, heads):
        ms, ls = carry
        off = chunk_off(c)
        new_ms, new_ls = [], []
        bias = sc_ref[pl.ds(off, TK), :]
        for n, h in enumerate(heads):
            lane0 = (h // 2) * PAIR
            q2 = qt_ref[0, h * PAIR:(h + 1) * PAIR, :]
            k2 = k_ref[0, pl.ds(off, TK), lane0:lane0 + PAIR]
            s8_ref[n] = jnp.dot(k2, q2, preferred_element_type=F32) + bias
        for n, h in enumerate(heads):
            rows_h = slice(h * HEAD_DIM, (h + 1) * HEAD_DIM)
            s = s8_ref[n]
            m_new = jnp.maximum(ms[n], jnp.max(s, axis=0, keepdims=True))
            p = jnp.exp2(s - m_new)
            alpha = jnp.exp2(ms[n] - m_new)
            new_ms.append(m_new)
            new_ls.append(alpha * ls[n] + jnp.sum(p, axis=0, keepdims=True))
            vc = vt_ref[0, rows_h, pl.ds(off, TK)]
            y_ref[rows_h, :] = alpha * y_ref[rows_h, :] + jnp.dot(
                vc, p.astype(BF16), preferred_element_type=F32)
        return tuple(new_ms), tuple(new_ls)

    ls = []
    for g in range(N_HEADS // HEADS_PER_PASS):
        heads = tuple(range(g * HEADS_PER_PASS, (g + 1) * HEADS_PER_PASS))
        init = (tuple(jnp.full((1, TQ), NEG, F32) for _ in heads),
                tuple(jnp.zeros((1, TQ), F32) for _ in heads))
        _, ls_g = lax.fori_loop(0, n_chunks, functools.partial(att_body, heads=heads), init)
        ls += list(ls_g)
    for h in range(N_HEADS):
        rows_h = slice(h * HEAD_DIM, (h + 1) * HEAD_DIM)
        out = y_ref[rows_h, :] / ls[h]
        ms_h = jnp.mean(out * out, axis=0, keepdims=True)
        y_ref[rows_h, :] = (out * lax.rsqrt(ms_h + EPS)) * gna_ref[rows_h, :]

    ya = (y_ref[...].T * gz_ref[0]).astype(BF16)
    yo = (jnp.dot(yconv_ref[0], wout_ref[0:D_CONV, :], preferred_element_type=F32)
          + jnp.dot(ya, wout_ref[D_CONV:D_CONV + D_ATTN, :], preferred_element_type=F32))
    gate = mod_ref[0][:, 2 * D_MODEL:3 * D_MODEL]
    o = x_ref[0] + gate * yo
    inv = lax.rsqrt(jnp.mean(o * o, axis=-1, keepdims=True) + EPS)
    o_ref[0] = (o * inv) * normf_ref[...]


def _attn_call(qt, qit, wt, k, ki, vt, gz, yconv, x, mod3, gn_attn_col, w_out_bf, norm_f):
    bsz, seq, _ = x.shape
    const = lambda b, i: (0, 0)
    rows = lambda b, i: (b, i, 0)
    cols = lambda b, i: (b, 0, i)
    whole = lambda b, i: (b, 0, 0)
    once = pl.Buffered(1)
    return pl.pallas_call(
        _attn_kernel,
        grid=(bsz, seq // TQ),
        in_specs=[
            pl.BlockSpec((1, N_HEADS * PAIR, TQ), cols),
            pl.BlockSpec((1, N_IDX_HEADS * PAIR, TQ), cols),
            pl.BlockSpec((1, N_IDX_HEADS, TQ), cols),
            pl.BlockSpec((1, seq, D_ATTN), whole),
            pl.BlockSpec((1, seq, PAIR), whole),
            pl.BlockSpec((1, D_ATTN, seq), whole),
            pl.BlockSpec((1, TQ, D_ATTN), rows),
            pl.BlockSpec((1, TQ, D_CONV), rows),
            pl.BlockSpec((1, TQ, D_MODEL), rows),
            pl.BlockSpec((1, 1, 3 * D_MODEL), whole),
            pl.BlockSpec((D_ATTN, 1), const),
            pl.BlockSpec((D_CONV + D_ATTN, D_MODEL), const, pipeline_mode=once),
            pl.BlockSpec((1, D_MODEL), const),
        ],
        out_specs=pl.BlockSpec((1, TQ, D_MODEL), rows),
        out_shape=jax.ShapeDtypeStruct((bsz, seq, D_MODEL), F32),
        scratch_shapes=[pltpu.VMEM((seq, TQ), F32), pltpu.VMEM((seq, TQ), BF16),
                        pltpu.VMEM((D_ATTN, TQ), F32),
                        pltpu.VMEM((HEADS_PER_PASS, TK, TQ), F32), pltpu.VMEM((1, TQ), F32)],
        compiler_params=pltpu.CompilerParams(dimension_semantics=("arbitrary", "arbitrary"),
                                             vmem_limit_bytes=VMEM_LIMIT),
        name="dsa_attention",
    )(qt, qit, wt, k, ki, vt, gz, yconv, x, mod3, gn_attn_col, w_out_bf, norm_f)


def _split_in_proj(w_in):
    conv = w_in[:, 0:2048]
    aq, ak, av, az = (w_in[:, 2048 + n * 512:2560 + n * 512] for n in range(4))
    iq = w_in[:, 4096:4608]
    ik = w_in[:, 4608:4672]
    iw = w_in[:, 4672:4680]
    w_nat = jnp.concatenate([conv, az], axis=1).astype(BF16)
    pad = jnp.zeros((D_MODEL, R_END - R_W - N_IDX_HEADS), w_in.dtype)
    w_cm = jnp.concatenate([aq, ak, av, iq, ik, ik, iw, pad], axis=1).T.astype(BF16)
    return w_nat, w_cm


def kernel(x, c, positions, w_ada, b_ada, norm_in, w_in, conv_w, conv_b, gn_conv, gn_attn, w_out, norm_f):
    bsz, seq, _ = x.shape
    mod3 = _ada_call(c, w_ada, b_ada).reshape(bsz, 1, 3 * D_MODEL)
    w_nat, w_cm = _split_in_proj(w_in)
    group = jnp.arange(D_CONV) // (D_CONV // CONV_GROUPS)
    gmat = jnp.where(group[:, None] == group[None, :], 1.0 / (D_CONV // CONV_GROUPS), 0.0).astype(BF16)
    yconv, gz, qt, k, vt, qit, ki, wt = _proj_call(
        x, positions.reshape(bsz, 1, seq), mod3, norm_in.reshape(1, D_MODEL), w_nat, w_cm,
        conv_w, conv_b.reshape(1, D_CONV), gn_conv.reshape(1, D_CONV), gmat)
    return _attn_call(qt, qit, wt, k, ki, vt, gz, yconv, x, mod3, gn_attn.reshape(D_ATTN, 1),
                      w_out.astype(BF16), norm_f.reshape(1, D_MODEL))
```

```python
import functools
import math

import jax
import jax.numpy as jnp
from jax import lax
from jax.experimental import pallas as pl
from jax.experimental.pallas import tpu as pltpu

D_MODEL = 1024
D_CONV = 512
D_ATTN = 512
HEAD_DIM = 64
N_HEADS = 8
CONV_GROUPS = 8
CONV_WIDTH = 3
ROPE_HALF = 8
ROPE_THETA = 500000.0
N_IDX_HEADS = 8
IDX_DIM = 64
TOPK = 256
EPS = 1e-6

LANES = 128
PAIR = 2 * HEAD_DIM
TS = 512
TQ = 512
TK = 512
HEADS_PER_PASS = 4
HALF = TQ // 2
assert TK == TQ and HALF >= TOPK
FINE_BITS = 17
NEG = -1e30
LOG2_E = math.log2(math.e)
F32_MAX = float(jnp.finfo(jnp.float32).max)
INT_MIN = -(2 ** 31)
KEY_LOWEST_FINITE = INT_MIN + 0x00800000
KEY_NEG_INF_BF16 = INT_MIN + 0x007F0000
VMEM_LIMIT = 56 * 1024 * 1024

F32 = jnp.float32
BF16 = jnp.bfloat16
I32 = jnp.int32


def _silu(v):
    return v * jax.nn.sigmoid(v)


def _ada_kernel(c_ref, w_ref, b_ref, o_ref):
    o_ref[...] = jnp.dot(_silu(c_ref[...]), w_ref[...], precision=lax.Precision.HIGHEST,
                         preferred_element_type=F32) + b_ref[...]


def _ada_call(c, w_ada, b_ada):
    bsz = c.shape[0]
    n_blk = 3
    return pl.pallas_call(
        _ada_kernel,
        grid=(n_blk,),
        in_specs=[pl.BlockSpec((bsz, D_MODEL), lambda n: (0, 0)),
                  pl.BlockSpec((D_MODEL, D_MODEL), lambda n: (0, n)),
                  pl.BlockSpec((1, D_MODEL), lambda n: (0, n))],
        out_specs=pl.BlockSpec((bsz, D_MODEL), lambda n: (0, n)),
        out_shape=jax.ShapeDtypeStruct((bsz, 3 * D_MODEL), F32),
        compiler_params=pltpu.CompilerParams(dimension_semantics=("arbitrary",)),
        name="adaln_mod",
    )(c, w_ada, b_ada.reshape(1, 3 * D_MODEL))


R_Q, R_K, R_V, R_QI, R_KI, R_W, R_END = 0, 512, 1024, 1536, 2048, 2176, 2192


def _rope_cm(xt, n_heads, cs, sn):
    t = xt.shape[-1]
    xr = xt.reshape(n_heads, HEAD_DIM, t)
    x1 = xr[:, 0:ROPE_HALF, :]
    x2 = xr[:, ROPE_HALF:2 * ROPE_HALF, :]
    r1 = x1 * cs - x2 * sn
    r2 = x1 * sn + x2 * cs
    return jnp.concatenate([r1, r2, xr[:, 2 * ROPE_HALF:, :]], axis=1).reshape(n_heads * HEAD_DIM, t)


def _pad_heads_to_pairs(xt):
    z = jnp.zeros((HEAD_DIM, xt.shape[-1]), xt.dtype)
    pieces = []
    for h in range(N_HEADS):
        blk = xt[h * HEAD_DIM:(h + 1) * HEAD_DIM, :]
        pieces += [blk, z] if h % 2 == 0 else [z, blk]
    return jnp.concatenate(pieces, axis=0)


def _proj_kernel(x_ref, pos_ref, mod_ref, normin_ref, wnat_ref, wcm_ref, convw_ref, convb_ref,
                 gnc_ref, gmat_ref,
                 yconv_ref, gz_ref, qt_ref, k_ref, vt_ref, qit_ref, ki_ref, wt_ref,
                 sbuf_ref):
    j = pl.program_id(1)
    x = x_ref[0]
    inv = lax.rsqrt(jnp.mean(x * x, axis=-1, keepdims=True) + EPS)
    mod = mod_ref[0]
    shift = mod[:, 0:D_MODEL]
    scale = mod[:, D_MODEL:2 * D_MODEL]
    h = (x * inv) * normin_ref[...] * (1.0 + scale) + shift
    hb = h.astype(BF16)

    nat = jnp.dot(hb, wnat_ref[...], preferred_element_type=F32)
    cb = nat[:, 0:512]
    cc = nat[:, 512:1024]
    cu = nat[:, 1024:1536]
    cz = nat[:, 1536:2048]
    az = nat[:, 2048:2560]
    gz_ref[0] = _silu(az)

    s = cc * cu

    @pl.when(j == 0)
    def _():
        sbuf_ref[0:8, :] = jnp.zeros((8, D_CONV), F32)

    sbuf_ref[8:TS + 8, :] = s
    s1 = sbuf_ref[7:TS + 7, :]
    s2 = sbuf_ref[6:TS + 6, :]
    cw = convw_ref[...]
    conv = convb_ref[...] + ((cw[0:1, :] * s2 + cw[1:2, :] * s1) + cw[2:3, :] * s)
    sbuf_ref[0:8, :] = sbuf_ref[TS:TS + 8, :]
    y = cb * conv
    ysq = y * y
    hi = ysq.astype(BF16)
    lo = (ysq - hi.astype(F32)).astype(BF16)
    gm = gmat_ref[...]
    ms = jnp.dot(hi, gm, preferred_element_type=F32) + jnp.dot(lo, gm, preferred_element_type=F32)
    yconv_ref[0] = ((y * lax.rsqrt(ms + EPS)) * gnc_ref[...] * _silu(cz)).astype(BF16)

    ht = h.T.astype(BF16)

    def proj_cm(r0, r1):
        return jnp.dot(wcm_ref[r0:r1, :], ht, preferred_element_type=F32)

    pos = pos_ref[0].astype(F32)
    jj = lax.broadcasted_iota(I32, (ROPE_HALF, 1), 0).astype(F32)
    theta = jnp.full((ROPE_HALF, 1), ROPE_THETA, F32)
    inv_freq = jnp.exp(jnp.log(theta) * (jj * (-1.0 / ROPE_HALF)))
    ang = inv_freq * pos
    cs = jnp.cos(ang)
    sn = jnp.sin(ang)

    scale_pow2 = HEAD_DIM ** -0.5
    q = _rope_cm(proj_cm(R_Q, R_K), N_HEADS, cs, sn) * (scale_pow2 * LOG2_E)
    qt_ref[0] = _pad_heads_to_pairs(q).astype(BF16)
    k = _rope_cm(proj_cm(R_K, R_V), N_HEADS, cs, sn)
    k_ref[0] = k.T.astype(BF16)
    vt_ref[0] = proj_cm(R_V, R_QI).astype(BF16)
    qi = _rope_cm(proj_cm(R_QI, R_KI), N_IDX_HEADS, cs, sn) * scale_pow2
    qit_ref[0] = _pad_heads_to_pairs(qi).astype(BF16)
    kiw = proj_cm(R_KI, R_END)
    ki2 = _rope_cm(kiw[0:PAIR, :], 2, cs, sn)
    ki_ref[0] = ki2.T.astype(BF16)
    wt_ref[0] = kiw[PAIR:PAIR + N_IDX_HEADS, :] * (N_IDX_HEADS ** -0.5)


def _proj_call(x, pos3, mod3, norm_in, w_nat, w_cm, conv_w, conv_b, gn_conv, gmat):
    bsz, seq, _ = x.shape
    const = lambda b, j: (0, 0)
    rows = lambda b, j: (b, j, 0)
    cols = lambda b, j: (b, 0, j)
    out_shapes = (
        jax.ShapeDtypeStruct((bsz, seq, D_CONV), BF16),
        jax.ShapeDtypeStruct((bsz, seq, D_ATTN), F32),
        jax.ShapeDtypeStruct((bsz, N_HEADS * PAIR, seq), BF16),
        jax.ShapeDtypeStruct((bsz, seq, D_ATTN), BF16),
        jax.ShapeDtypeStruct((bsz, D_ATTN, seq), BF16),
        jax.ShapeDtypeStruct((bsz, N_IDX_HEADS * PAIR, seq), BF16),
        jax.ShapeDtypeStruct((bsz, seq, PAIR), BF16),
        jax.ShapeDtypeStruct((bsz, N_IDX_HEADS, seq), F32),
    )
    return pl.pallas_call(
        _proj_kernel,
        grid=(bsz, seq // TS),
        in_specs=[
            pl.BlockSpec((1, TS, D_MODEL), rows),
            pl.BlockSpec((1, 1, TS), cols),
            pl.BlockSpec((1, 1, 3 * D_MODEL), lambda b, j: (b, 0, 0)),
            pl.BlockSpec((1, D_MODEL), const),
            pl.BlockSpec(w_nat.shape, const),
            pl.BlockSpec(w_cm.shape, const),
            pl.BlockSpec((CONV_WIDTH, D_CONV), const),
            pl.BlockSpec((1, D_CONV), const),
            pl.BlockSpec((1, D_CONV), const),
            pl.BlockSpec((D_CONV, D_CONV), const),
        ],
        out_specs=(
            pl.BlockSpec((1, TS, D_CONV), rows),
            pl.BlockSpec((1, TS, D_ATTN), rows),
            pl.BlockSpec((1, N_HEADS * PAIR, TS), cols),
            pl.BlockSpec((1, TS, D_ATTN), rows),
            pl.BlockSpec((1, D_ATTN, TS), cols),
            pl.BlockSpec((1, N_IDX_HEADS * PAIR, TS), cols),
            pl.BlockSpec((1, TS, PAIR), rows),
            pl.BlockSpec((1, N_IDX_HEADS, TS), cols),
        ),
        out_shape=out_shapes,
        scratch_shapes=[pltpu.VMEM((TS + 8, D_CONV), F32)],
        compiler_params=pltpu.CompilerParams(dimension_semantics=("arbitrary", "arbitrary"),
                                             vmem_limit_bytes=VMEM_LIMIT),
        name="in_proj",
    )(x, pos3, mod3, norm_in, w_nat, w_cm, conv_w, conv_b, gn_conv, gmat)


def _key_to_f32(key):
    bits = jnp.where(key < 0, key ^ 0x7FFFFFFF, key)
    return lax.bitcast_convert_type(bits, F32)


def _attn_kernel(qt_ref, qit_ref, wt_ref, k_ref, ki_ref, vt_ref, gz_ref, yconv_ref, x_ref, mod_ref,
                 gna_ref, wout_ref, normf_ref, o_ref, sc_ref, sb_ref, y_ref, s8_ref, thr_ref):
    i = pl.program_id(1)
    n_chunks = i + 1
    row = lax.broadcasted_iota(I32, (TK, TQ), 0)
    col = lax.broadcasted_iota(I32, (TK, TQ), 1)

    def chunk_off(c):
        return pl.multiple_of(c * TK, TK)

    def score_tile(key0, n_keys, lane0, n_lanes):
        lanes = slice(lane0, lane0 + n_lanes)
        kc = ki_ref[0, pl.ds(key0, n_keys), :]
        acc = jnp.zeros((n_keys, n_lanes), F32)
        for hh in range(N_IDX_HEADS):
            d = jnp.dot(kc, qit_ref[0, hh * PAIR:(hh + 1) * PAIR, lanes], preferred_element_type=F32)
            acc = acc + wt_ref[0, hh:hh + 1, lanes] * jnp.maximum(d, 0.0)
        return acc

    def score_chunk(c):
        return score_tile(chunk_off(c), TK, 0, TQ)

    def score_diagonal():
        d0 = chunk_off(i)
        top = jnp.where(lax.broadcasted_iota(I32, (HALF, TQ), 0) <= lax.broadcasted_iota(I32, (HALF, TQ), 1),
                        score_tile(d0, HALF, 0, TQ), -jnp.inf)
        corner = jnp.where(
            lax.broadcasted_iota(I32, (HALF, HALF), 0) <= lax.broadcasted_iota(I32, (HALF, HALF), 1),
            score_tile(pl.multiple_of(d0 + HALF, HALF), HALF, HALF, HALF), -jnp.inf)
        bottom = jnp.concatenate([jnp.full((HALF, HALF), -jnp.inf, F32), corner], axis=1)
        return jnp.concatenate([top, bottom], axis=0)

    def put_scores(c, tile):
        sc_ref[pl.ds(chunk_off(c), TK), :] = tile
        sb_ref[pl.ds(chunk_off(c), TK), :] = tile.astype(BF16)

    def score_body(c, carry):
        put_scores(c, score_chunk(c))
        return carry

    lax.fori_loop(0, i, score_body, 0)
    put_scores(i, score_diagonal())

    def swept_counts(src_ref, rows_per_tile, tile_hits):
        def body(c, acc):
            return acc + tile_hits(src_ref[pl.ds(chunk_off(c), TK), :], slice(None))
        acc = lax.fori_loop(0, i, body, jnp.zeros((rows_per_tile, TQ), F32))
        d0 = chunk_off(i)
        acc = acc + tile_hits(src_ref[pl.ds(d0, HALF), :], slice(None))
        right = tile_hits(src_ref[pl.ds(pl.multiple_of(d0 + HALF, HALF), HALF), HALF:],
                          slice(HALF, None))
        return acc + jnp.concatenate([jnp.zeros((rows_per_tile, HALF), F32), right], axis=1)

    def count_ge(thr):
        def tile_hits(tile, lanes):
            hit = jnp.where(tile >= thr[:, lanes], 1.0, 0.0)
            return jnp.sum(hit.reshape(tile.shape[0] // 8, 8, tile.shape[1]), axis=0)
        return jnp.sum(swept_counts(sc_ref, 8, tile_hits), axis=0, keepdims=True).astype(I32)

    def count_ge_bf16(thr_b):
        def tile_hits(tile, lanes):
            hit = jnp.where(tile >= thr_b[:, lanes], jnp.ones((), BF16), jnp.zeros((), BF16))
            hit = hit.reshape(tile.shape[0] // 16, 16, tile.shape[1])
            part = hit[0]
            for r in range(1, tile.shape[0] // 16):
                part = part + hit[r]
            return part.astype(F32)
        return jnp.sum(swept_counts(sb_ref, 16, tile_hits), axis=0, keepdims=True).astype(I32)

    def coarse_body(b, lo_u):
        cand_u = lo_u | lax.shift_left(jnp.int32(1), 31 - b)
        key = cand_u ^ INT_MIN
        bits = jnp.where(key < 0, key ^ 0x7FFF0000, key)
        cand = lax.bitcast_convert_type(bits, F32).astype(BF16)
        cnt = jnp.where(key < KEY_NEG_INF_BF16, TOPK, count_ge_bf16(cand))
        return jnp.where(cnt >= TOPK, cand_u, lo_u)

    coarse_u = lax.fori_loop(0, 16, coarse_body, jnp.zeros((1, TQ), I32))
    coarse_key = coarse_u ^ INT_MIN
    coarse_key = jnp.where(coarse_key < 0, coarse_key | 0xFFFF, coarse_key)
    base_key = coarse_key - (1 << (FINE_BITS - 1))

    def fine_body(b, carry):
        off, cnt_lo = carry
        cand_off = off | lax.shift_left(jnp.int32(1), FINE_BITS - 1 - b)
        cnt = count_ge(_key_to_f32(base_key + cand_off))
        ok = cnt >= TOPK
        return jnp.where(ok, cand_off, off), jnp.where(ok, cnt, cnt_lo)

    off, cnt_lo = lax.fori_loop(0, FINE_BITS, fine_body,
                                (jnp.zeros((1, TQ), I32), jnp.zeros((1, TQ), I32)))
    lo = base_key + off
    has_kth = lo >= KEY_LOWEST_FINITE
    thr0 = jnp.where(has_kth, _key_to_f32(lo), -F32_MAX)
    tie0 = jnp.where(has_kth, cnt_lo, 0) > TOPK

    thr_ref[...] = thr0

    @pl.when(jnp.max(tie0.astype(I32)) > 0)
    def _():
        def midpoint(lo_f, hi_f):
            return lo_f + (hi_f - lo_f) * 0.5

        def splits(lo_f, hi_f):
            mid = midpoint(lo_f, hi_f)
            return jnp.where(mid > lo_f, jnp.where(mid < hi_f, 1, 0), 0).astype(I32)

        def refine_cond(carry):
            lo_f, hi_f, _, it = carry
            return jnp.logical_and(jnp.max(splits(lo_f, hi_f)) > 0, it < 64)

        def refine_body(carry):
            lo_f, hi_f, cnt_f, it = carry
            mid = midpoint(lo_f, hi_f)
            cnt = count_ge(mid)
            up = jnp.where(splits(lo_f, hi_f) > 0, jnp.where(cnt >= TOPK, 1, 0), 0) > 0
            down = jnp.where(splits(lo_f, hi_f) > 0, jnp.where(cnt >= TOPK, 0, 1), 0) > 0
            return (jnp.where(up, mid, lo_f), jnp.where(down, mid, hi_f),
                    jnp.where(up, cnt, cnt_f), it + 1)

        above0 = jnp.where(tie0, _key_to_f32(lo + 1), thr0)
        thr, _, cnt_thr, _ = lax.while_loop(refine_cond, refine_body,
                                            (thr0, above0, cnt_lo, jnp.int32(0)))
        thr_ref[...] = thr

        excess = jnp.where(jnp.where(tie0, cnt_thr, 0) > TOPK, cnt_thr - TOPK, 0).astype(F32)
        suffix = jnp.where(row <= col, 1.0, 0.0).astype(BF16)

        def drop_body(j, later):
            off = chunk_off(n_chunks - 1 - j)
            tile = sc_ref[pl.ds(off, TK), :]
            tied = tile == thr
            from_end = later + jnp.dot(suffix, jnp.where(tied, 1.0, 0.0).astype(BF16),
                                       preferred_element_type=F32)
            dropped = jnp.where(from_end <= excess, -jnp.inf, tile)
            sc_ref[pl.ds(off, TK), :] = jnp.where(tied, dropped, tile)
            return from_end[0:1, :]

        lax.fori_loop(0, n_chunks, drop_body, jnp.zeros((1, TQ), F32))

    thr = thr_ref[...]

    def bias_body(c, carry):
        tile = sc_ref[pl.ds(chunk_off(c), TK), :]
        sc_ref[pl.ds(chunk_off(c), TK), :] = jnp.where(tile >= thr, 0.0, NEG)
        return carry

    lax.fori_loop(0, n_chunks, bias_body, 0)

    y_ref[...] = jnp.zeros((D_ATTN, TQ), F32)

    def att_body(c, carry, heads):
        ms, ls = carry
        off = chunk_off(c)
        new_ms, new_ls = [], []
        bias = sc_ref[pl.ds(off, TK), :]
        for n, h in enumerate(heads):
            lane0 = (h // 2) * PAIR
            q2 = qt_ref[0, h * PAIR:(h + 1) * PAIR, :]
            k2 = k_ref[0, pl.ds(off, TK), lane0:lane0 + PAIR]
            s8_ref[n] = jnp.dot(k2, q2, preferred_element_type=F32) + bias
        for n, h in enumerate(heads):
            rows_h = slice(h * HEAD_DIM, (h + 1) * HEAD_DIM)
            s = s8_ref[n]
            m_new = jnp.maximum(ms[n], jnp.max(s, axis=0, keepdims=True))
            p = jnp.exp2(s - m_new)
            alpha = jnp.exp2(ms[n] - m_new)
            new_ms.append(m_new)
            new_ls.append(alpha * ls[n] + jnp.sum(p, axis=0, keepdims=True))
            vc = vt_ref[0, rows_h, pl.ds(off, TK)]
            y_ref[rows_h, :] = alpha * y_ref[rows_h, :] + jnp.dot(
                vc, p.astype(BF16), preferred_element_type=F32)
        return tuple(new_ms), tuple(new_ls)

    ls = []
    for g in range(N_HEADS // HEADS_PER_PASS):
        heads = tuple(range(g * HEADS_PER_PASS, (g + 1) * HEADS_PER_PASS))
        init = (tuple(jnp.full((1, TQ), NEG, F32) for _ in heads),
                tuple(jnp.zeros((1, TQ), F32) for _ in heads))
        _, ls_g = lax.fori_loop(0, n_chunks, functools.partial(att_body, heads=heads), init)
        ls += list(ls_g)
    for h in range(N_HEADS):
        rows_h = slice(h * HEAD_DIM, (h + 1) * HEAD_DIM)
        out = y_ref[rows_h, :] / ls[h]
        ms_h = jnp.mean(out * out, axis=0, keepdims=True)
        y_ref[rows_h, :] = (out * lax.rsqrt(ms_h + EPS)) * gna_ref[rows_h, :]

    ya = (y_ref[...].T * gz_ref[0]).astype(BF16)
    yo = (jnp.dot(yconv_ref[0], wout_ref[0:D_CONV, :], preferred_element_type=F32)
          + jnp.dot(ya, wout_ref[D_CONV:D_CONV + D_ATTN, :], preferred_element_type=F32))
    gate = mod_ref[0][:, 2 * D_MODEL:3 * D_MODEL]
    o = x_ref[0] + gate * yo
    inv = lax.rsqrt(jnp.mean(o * o, axis=-1, keepdims=True) + EPS)
    o_ref[0] = (o * inv) * normf_ref[...]


def _attn_call(qt, qit, wt, k, ki, vt, gz, yconv, x, mod3, gn_attn_col, w_out_bf, norm_f):
    bsz, seq, _ = x.shape
    const = lambda b, i: (0, 0)
    rows = lambda b, i: (b, i, 0)
    cols = lambda b, i: (b, 0, i)
    whole = lambda b, i: (b, 0, 0)
    once = pl.Buffered(1)
    return pl.pallas_call(
        _attn_kernel,
        grid=(bsz, seq // TQ),
        in_specs=[
            pl.BlockSpec((1, N_HEADS * PAIR, TQ), cols),
            pl.BlockSpec((1, N_IDX_HEADS * PAIR, TQ), cols),
            pl.BlockSpec((1, N_IDX_HEADS, TQ), cols),
            pl.BlockSpec((1, seq, D_ATTN), whole),
            pl.BlockSpec((1, seq, PAIR), whole),
            pl.BlockSpec((1, D_ATTN, seq), whole),
            pl.BlockSpec((1, TQ, D_ATTN), rows),
            pl.BlockSpec((1, TQ, D_CONV), rows),
            pl.BlockSpec((1, TQ, D_MODEL), rows),
            pl.BlockSpec((1, 1, 3 * D_MODEL), whole),
            pl.BlockSpec((D_ATTN, 1), const),
            pl.BlockSpec((D_CONV + D_ATTN, D_MODEL), const, pipeline_mode=once),
            pl.BlockSpec((1, D_MODEL), const),
        ],
        out_specs=pl.BlockSpec((1, TQ, D_MODEL), rows),
        out_shape=jax.ShapeDtypeStruct((bsz, seq, D_MODEL), F32),
        scratch_shapes=[pltpu.VMEM((seq, TQ), F32), pltpu.VMEM((seq, TQ), BF16),
                        pltpu.VMEM((D_ATTN, TQ), F32),
                        pltpu.VMEM((HEADS_PER_PASS, TK, TQ), F32), pltpu.VMEM((1, TQ), F32)],
        compiler_params=pltpu.CompilerParams(dimension_semantics=("arbitrary", "arbitrary"),
                                             vmem_limit_bytes=VMEM_LIMIT),
        name="dsa_attention",
    )(qt, qit, wt, k, ki, vt, gz, yconv, x, mod3, gn_attn_col, w_out_bf, norm_f)


def _split_in_proj(w_in):
    conv = w_in[:, 0:2048]
    aq, ak, av, az = (w_in[:, 2048 + n * 512:2560 + n * 512] for n in range(4))
    iq = w_in[:, 4096:4608]
    ik = w_in[:, 4608:4672]
    iw = w_in[:, 4672:4680]
    w_nat = jnp.concatenate([conv, az], axis=1).astype(BF16)
    pad = jnp.zeros((D_MODEL, R_END - R_W - N_IDX_HEADS), w_in.dtype)
    w_cm = jnp.concatenate([aq, ak, av, iq, ik, ik, iw, pad], axis=1).T.astype(BF16)
    return w_nat, w_cm


def kernel(x, c, positions, w_ada, b_ada, norm_in, w_in, conv_w, conv_b, gn_conv, gn_attn, w_out, norm_f):
    bsz, seq, _ = x.shape
    mod3 = _ada_call(c, w_ada, b_ada).reshape(bsz, 1, 3 * D_MODEL)
    w_nat, w_cm = _split_in_proj(w_in)
    group = jnp.arange(D_CONV) // (D_CONV // CONV_GROUPS)
    gmat = jnp.where(group[:, None] == group[None, :], 1.0 / (D_CONV // CONV_GROUPS), 0.0).astype(BF16)
    yconv, gz, qt, k, vt, qit, ki, wt = _proj_call(
        x, positions.reshape(bsz, 1, seq), mod3, norm_in.reshape(1, D_MODEL), w_nat, w_cm,
        conv_w, conv_b.reshape(1, D_CONV), gn_conv.reshape(1, D_CONV), gmat)
    return _attn_call(qt, qit, wt, k, ki, vt, gz, yconv, x, mod3, gn_attn.reshape(D_ATTN, 1),
                      w_out.astype(BF16), norm_f.reshape(1, D_MODEL))
```

```python
import functools
import math

import jax
import jax.numpy as jnp
from jax import lax
from jax.experimental import pallas as pl
from jax.experimental.pallas import tpu as pltpu

D_MODEL = 1024
D_CONV = 512
D_ATTN = 512
HEAD_DIM = 64
N_HEADS = 8
CONV_GROUPS = 8
CONV_WIDTH = 3
ROPE_HALF = 8
ROPE_THETA = 500000.0
N_IDX_HEADS = 8
IDX_DIM = 64
TOPK = 256
EPS = 1e-6

LANES = 128
PAIR = 2 * HEAD_DIM
TS = 512
TQ = 512
TK = 512
HEADS_PER_PASS = 4
HALF = TQ // 2
assert TK == TQ and HALF >= TOPK
FINE_BITS = 17
NEG = -1e30
LOG2_E = math.log2(math.e)
F32_MAX = float(jnp.finfo(jnp.float32).max)
INT_MIN = -(2 ** 31)
KEY_LOWEST_FINITE = INT_MIN + 0x00800000
KEY_NEG_INF_BF16 = INT_MIN + 0x007F0000
VMEM_LIMIT = 56 * 1024 * 1024

F32 = jnp.float32
BF16 = jnp.bfloat16
I32 = jnp.int32


def _silu(v):
    return v * jax.nn.sigmoid(v)


def _ada_kernel(c_ref, w_ref, b_ref, o_ref):
    o_ref[...] = jnp.dot(_silu(c_ref[...]), w_ref[...], precision=lax.Precision.HIGHEST,
                         preferred_element_type=F32) + b_ref[...]


def _ada_call(c, w_ada, b_ada):
    bsz = c.shape[0]
    n_blk = 3
    return pl.pallas_call(
        _ada_kernel,
        grid=(n_blk,),
        in_specs=[pl.BlockSpec((bsz, D_MODEL), lambda n: (0, 0)),
                  pl.BlockSpec((D_MODEL, D_MODEL), lambda n: (0, n)),
                  pl.BlockSpec((1, D_MODEL), lambda n: (0, n))],
        out_specs=pl.BlockSpec((bsz, D_MODEL), lambda n: (0, n)),
        out_shape=jax.ShapeDtypeStruct((bsz, 3 * D_MODEL), F32),
        compiler_params=pltpu.CompilerParams(dimension_semantics=("arbitrary",)),
        name="adaln_mod",
    )(c, w_ada, b_ada.reshape(1, 3 * D_MODEL))


R_Q, R_K, R_V, R_QI, R_KI, R_W, R_END = 0, 512, 1024, 1536, 2048, 2176, 2192


def _rope_cm(xt, n_heads, cs, sn):
    t = xt.shape[-1]
    xr = xt.reshape(n_heads, HEAD_DIM, t)
    x1 = xr[:, 0:ROPE_HALF, :]
    x2 = xr[:, ROPE_HALF:2 * ROPE_HALF, :]
    r1 = x1 * cs - x2 * sn
    r2 = x1 * sn + x2 * cs
    return jnp.concatenate([r1, r2, xr[:, 2 * ROPE_HALF:, :]], axis=1).reshape(n_heads * HEAD_DIM, t)


def _pad_heads_to_pairs(xt):
    z = jnp.zeros((HEAD_DIM, xt.shape[-1]), xt.dtype)
    pieces = []
    for h in range(N_HEADS):
        blk = xt[h * HEAD_DIM:(h + 1) * HEAD_DIM, :]
        pieces += [blk, z] if h % 2 == 0 else [z, blk]
    return jnp.concatenate(pieces, axis=0)


def _proj_kernel(x_ref, pos_ref, mod_ref, normin_ref, wnat_ref, wcm_ref, convw_ref, convb_ref,
                 gnc_ref, gmat_ref,
                 yconv_ref, gz_ref, qt_ref, k_ref, vt_ref, qit_ref, ki_ref, wt_ref,
                 sbuf_ref):
    j = pl.program_id(1)
    x = x_ref[0]
    inv = lax.rsqrt(jnp.mean(x * x, axis=-1, keepdims=True) + EPS)
    mod = mod_ref[0]
    shift = mod[:, 0:D_MODEL]
    scale = mod[:, D_MODEL:2 * D_MODEL]
    h = (x * inv) * normin_ref[...] * (1.0 + scale) + shift
    hb = h.astype(BF16)

    gz_ref[0] = _silu(jnp.dot(hb, wnat_ref[:, 4 * D_CONV:5 * D_CONV], preferred_element_type=F32))

    @pl.when(j == 0)
    def _():
        sbuf_ref[0:8, :] = jnp.zeros((8, D_CONV), F32)

    half_w = D_CONV // 2
    for part in range(2):
        c0 = part * half_w
        cols = slice(c0, c0 + half_w)

        def proj_nat(seg):
            return jnp.dot(hb, wnat_ref[:, seg * D_CONV + c0:seg * D_CONV + c0 + half_w],
                           preferred_element_type=F32)
        cb, cc, cu, cz = proj_nat(0), proj_nat(1), proj_nat(2), proj_nat(3)
        s = cc * cu
        sbuf_ref[8:TS + 8, cols] = s
        s1 = sbuf_ref[7:TS + 7, cols]
        s2 = sbuf_ref[6:TS + 6, cols]
        conv = convb_ref[:, cols] + ((convw_ref[0:1, cols] * s2 + convw_ref[1:2, cols] * s1)
                                     + convw_ref[2:3, cols] * s)
        sbuf_ref[0:8, cols] = sbuf_ref[TS:TS + 8, cols]
        y = cb * conv
        ysq = y * y
        hi = ysq.astype(BF16)
        lo = (ysq - hi.astype(F32)).astype(BF16)
        gm = gmat_ref[cols, cols]
        ms = jnp.dot(hi, gm, preferred_element_type=F32) + jnp.dot(lo, gm, preferred_element_type=F32)
        yconv_ref[0, :, cols] = ((y * lax.rsqrt(ms + EPS)) * gnc_ref[:, cols] * _silu(cz)).astype(BF16)

    ht = h.T.astype(BF16)

    def proj_cm(r0, r1):
        return jnp.dot(wcm_ref[r0:r1, :], ht, preferred_element_type=F32)

    pos = pos_ref[0].astype(F32)
    jj = lax.broadcasted_iota(I32, (ROPE_HALF, 1), 0).astype(F32)
    theta = jnp.full((ROPE_HALF, 1), ROPE_THETA, F32)
    inv_freq = jnp.exp(jnp.log(theta) * (jj * (-1.0 / ROPE_HALF)))
    ang = inv_freq * pos
    cs = jnp.cos(ang)
    sn = jnp.sin(ang)

    scale_pow2 = HEAD_DIM ** -0.5
    q = _rope_cm(proj_cm(R_Q, R_K), N_HEADS, cs, sn) * (scale_pow2 * LOG2_E)
    qt_ref[0] = _pad_heads_to_pairs(q).astype(BF16)
    k = _rope_cm(proj_cm(R_K, R_V), N_HEADS, cs, sn)
    k_ref[0] = k.T.astype(BF16)
    vt_ref[0] = proj_cm(R_V, R_QI).astype(BF16)
    qi = _rope_cm(proj_cm(R_QI, R_KI), N_IDX_HEADS, cs, sn) * scale_pow2
    qit_ref[0] = _pad_heads_to_pairs(qi).astype(BF16)
    kiw = proj_cm(R_KI, R_END)
    ki2 = _rope_cm(kiw[0:PAIR, :], 2, cs, sn)
    ki_ref[0] = ki2.T.astype(BF16)
    wt_ref[0] = kiw[PAIR:PAIR + N_IDX_HEADS, :] * (N_IDX_HEADS ** -0.5)


def _proj_call(x, pos3, mod3, norm_in, w_nat, w_cm, conv_w, conv_b, gn_conv, gmat):
    bsz, seq, _ = x.shape
    const = lambda b, j: (0, 0)
    rows = lambda b, j: (b, j, 0)
    cols = lambda b, j: (b, 0, j)
    out_shapes = (
        jax.ShapeDtypeStruct((bsz, seq, D_CONV), BF16),
        jax.ShapeDtypeStruct((bsz, seq, D_ATTN), F32),
        jax.ShapeDtypeStruct((bsz, N_HEADS * PAIR, seq), BF16),
        jax.ShapeDtypeStruct((bsz, seq, D_ATTN), BF16),
        jax.ShapeDtypeStruct((bsz, D_ATTN, seq), BF16),
        jax.ShapeDtypeStruct((bsz, N_IDX_HEADS * PAIR, seq), BF16),
        jax.ShapeDtypeStruct((bsz, seq, PAIR), BF16),
        jax.ShapeDtypeStruct((bsz, N_IDX_HEADS, seq), F32),
    )
    return pl.pallas_call(
        _proj_kernel,
        grid=(bsz, seq // TS),
        in_specs=[
            pl.BlockSpec((1, TS, D_MODEL), rows),
            pl.BlockSpec((1, 1, TS), cols),
            pl.BlockSpec((1, 1, 3 * D_MODEL), lambda b, j: (b, 0, 0)),
            pl.BlockSpec((1, D_MODEL), const),
            pl.BlockSpec(w_nat.shape, const),
            pl.BlockSpec(w_cm.shape, const),
            pl.BlockSpec((CONV_WIDTH, D_CONV), const),
            pl.BlockSpec((1, D_CONV), const),
            pl.BlockSpec((1, D_CONV), const),
            pl.BlockSpec((D_CONV, D_CONV), const),
        ],
        out_specs=(
            pl.BlockSpec((1, TS, D_CONV), rows),
            pl.BlockSpec((1, TS, D_ATTN), rows),
            pl.BlockSpec((1, N_HEADS * PAIR, TS), cols),
            pl.BlockSpec((1, TS, D_ATTN), rows),
            pl.BlockSpec((1, D_ATTN, TS), cols),
            pl.BlockSpec((1, N_IDX_HEADS * PAIR, TS), cols),
            pl.BlockSpec((1, TS, PAIR), rows),
            pl.BlockSpec((1, N_IDX_HEADS, TS), cols),
        ),
        out_shape=out_shapes,
        scratch_shapes=[pltpu.VMEM((TS + 8, D_CONV), F32)],
        compiler_params=pltpu.CompilerParams(dimension_semantics=("arbitrary", "arbitrary"),
                                             vmem_limit_bytes=VMEM_LIMIT),
        name="in_proj",
    )(x, pos3, mod3, norm_in, w_nat, w_cm, conv_w, conv_b, gn_conv, gmat)


def _key_to_f32(key):
    bits = jnp.where(key < 0, key ^ 0x7FFFFFFF, key)
    return lax.bitcast_convert_type(bits, F32)


def _attn_kernel(qt_ref, qit_ref, wt_ref, k_ref, ki_ref, vt_ref, gz_ref, yconv_ref, x_ref, mod_ref,
                 gna_ref, wout_ref, normf_ref, o_ref, sc_ref, sb_ref, y_ref, s8_ref, thr_ref):
    i = pl.program_id(1)
    n_chunks = i + 1
    row = lax.broadcasted_iota(I32, (TK, TQ), 0)
    col = lax.broadcasted_iota(I32, (TK, TQ), 1)

    def chunk_off(c):
        return pl.multiple_of(c * TK, TK)

    def score_tile(key0, n_keys, lane0, n_lanes):
        lanes = slice(lane0, lane0 + n_lanes)
        kc = ki_ref[0, pl.ds(key0, n_keys), :]
        acc = jnp.zeros((n_keys, n_lanes), F32)
        for hh in range(N_IDX_HEADS):
            d = jnp.dot(kc, qit_ref[0, hh * PAIR:(hh + 1) * PAIR, lanes], preferred_element_type=F32)
            acc = acc + wt_ref[0, hh:hh + 1, lanes] * jnp.maximum(d, 0.0)
        return acc

    def score_chunk(c):
        return score_tile(chunk_off(c), TK, 0, TQ)

    def score_diagonal():
        d0 = chunk_off(i)
        top = jnp.where(lax.broadcasted_iota(I32, (HALF, TQ), 0) <= lax.broadcasted_iota(I32, (HALF, TQ), 1),
                        score_tile(d0, HALF, 0, TQ), -jnp.inf)
        corner = jnp.where(
            lax.broadcasted_iota(I32, (HALF, HALF), 0) <= lax.broadcasted_iota(I32, (HALF, HALF), 1),
            score_tile(pl.multiple_of(d0 + HALF, HALF), HALF, HALF, HALF), -jnp.inf)
        bottom = jnp.concatenate([jnp.full((HALF, HALF), -jnp.inf, F32), corner], axis=1)
        return jnp.concatenate([top, bottom], axis=0)

    def put_scores(c, tile):
        sc_ref[pl.ds(chunk_off(c), TK), :] = tile
        sb_ref[pl.ds(chunk_off(c), TK), :] = tile.astype(BF16)

    def score_body(c, carry):
        put_scores(c, score_chunk(c))
        return carry

    lax.fori_loop(0, i, score_body, 0)
    put_scores(i, score_diagonal())

    def swept_counts(src_ref, rows_per_tile, tile_hits):
        def body(c, acc):
            return acc + tile_hits(src_ref[pl.ds(chunk_off(c), TK), :], slice(None))
        acc = lax.fori_loop(0, i, body, jnp.zeros((rows_per_tile, TQ), F32))
        d0 = chunk_off(i)
        acc = acc + tile_hits(src_ref[pl.ds(d0, HALF), :], slice(None))
        right = tile_hits(src_ref[pl.ds(pl.multiple_of(d0 + HALF, HALF), HALF), HALF:],
                          slice(HALF, None))
        return acc + jnp.concatenate([jnp.zeros((rows_per_tile, HALF), F32), right], axis=1)

    def count_ge(thr):
        def tile_hits(tile, lanes):
            hit = jnp.where(tile >= thr[:, lanes], 1.0, 0.0)
            return jnp.sum(hit.reshape(tile.shape[0] // 8, 8, tile.shape[1]), axis=0)
        return jnp.sum(swept_counts(sc_ref, 8, tile_hits), axis=0, keepdims=True).astype(I32)

    def count_ge_bf16(thr_b):
        def tile_hits(tile, lanes):
            hit = jnp.where(tile >= thr_b[:, lanes], jnp.ones((), BF16), jnp.zeros((), BF16))
            hit = hit.reshape(tile.shape[0] // 16, 16, tile.shape[1])
            part = hit[0]
            for r in range(1, tile.shape[0] // 16):
                part = part + hit[r]
            return part.astype(F32)
        return jnp.sum(swept_counts(sb_ref, 16, tile_hits), axis=0, keepdims=True).astype(I32)

    def coarse_body(b, lo_u):
        cand_u = lo_u | lax.shift_left(jnp.int32(1), 31 - b)
        key = cand_u ^ INT_MIN
        bits = jnp.where(key < 0, key ^ 0x7FFF0000, key)
        cand = lax.bitcast_convert_type(bits, F32).astype(BF16)
        cnt = jnp.where(key < KEY_NEG_INF_BF16, TOPK, count_ge_bf16(cand))
        return jnp.where(cnt >= TOPK, cand_u, lo_u)

    coarse_u = lax.fori_loop(0, 16, coarse_body, jnp.zeros((1, TQ), I32))
    coarse_key = coarse_u ^ INT_MIN
    coarse_key = jnp.where(coarse_key < 0, coarse_key | 0xFFFF, coarse_key)
    base_key = coarse_key - (1 << (FINE_BITS - 1))

    def fine_body(b, carry):
        off, cnt_lo = carry
        cand_off = off | lax.shift_left(jnp.int32(1), FINE_BITS - 1 - b)
        cnt = count_ge(_key_to_f32(base_key + cand_off))
        ok = cnt >= TOPK
        return jnp.where(ok, cand_off, off), jnp.where(ok, cnt, cnt_lo)

    off, cnt_lo = lax.fori_loop(0, FINE_BITS, fine_body,
                                (jnp.zeros((1, TQ), I32), jnp.zeros((1, TQ), I32)))
    lo = base_key + off
    has_kth = lo >= KEY_LOWEST_FINITE
    thr0 = jnp.where(has_kth, _key_to_f32(lo), -F32_MAX)
    tie0 = jnp.where(has_kth, cnt_lo, 0) > TOPK

    thr_ref[...] = thr0

    @pl.when(jnp.max(tie0.astype(I32)) > 0)
    def _():
        def midpoint(lo_f, hi_f):
            return lo_f + (hi_f - lo_f) * 0.5

        def splits(lo_f, hi_f):
            mid = midpoint(lo_f, hi_f)
            return jnp.where(mid > lo_f, jnp.where(mid < hi_f, 1, 0), 0).astype(I32)

        def refine_cond(carry):
            lo_f, hi_f, _, it = carry
            return jnp.logical_and(jnp.max(splits(lo_f, hi_f)) > 0, it < 64)

        def refine_body(carry):
            lo_f, hi_f, cnt_f, it = carry
            mid = midpoint(lo_f, hi_f)
            cnt = count_ge(mid)
            up = jnp.where(splits(lo_f, hi_f) > 0, jnp.where(cnt >= TOPK, 1, 0), 0) > 0
            down = jnp.where(splits(lo_f, hi_f) > 0, jnp.where(cnt >= TOPK, 0, 1), 0) > 0
            return (jnp.where(up, mid, lo_f), jnp.where(down, mid, hi_f),
                    jnp.where(up, cnt, cnt_f), it + 1)

        above0 = jnp.where(tie0, _key_to_f32(lo + 1), thr0)
        thr, _, cnt_thr, _ = lax.while_loop(refine_cond, refine_body,
                                            (thr0, above0, cnt_lo, jnp.int32(0)))
        thr_ref[...] = thr

        excess = jnp.where(jnp.where(tie0, cnt_thr, 0) > TOPK, cnt_thr - TOPK, 0).astype(F32)
        suffix = jnp.where(row <= col, 1.0, 0.0).astype(BF16)

        def drop_body(j, later):
            off = chunk_off(n_chunks - 1 - j)
            tile = sc_ref[pl.ds(off, TK), :]
            tied = tile == thr
            from_end = later + jnp.dot(suffix, jnp.where(tied, 1.0, 0.0).astype(BF16),
                                       preferred_element_type=F32)
            dropped = jnp.where(from_end <= excess, -jnp.inf, tile)
            sc_ref[pl.ds(off, TK), :] = jnp.where(tied, dropped, tile)
            return from_end[0:1, :]

        lax.fori_loop(0, n_chunks, drop_body, jnp.zeros((1, TQ), F32))

    thr = thr_ref[...]

    def bias_body(c, carry):
        tile = sc_ref[pl.ds(chunk_off(c), TK), :]
        sc_ref[pl.ds(chunk_off(c), TK), :] = jnp.where(tile >= thr, 0.0, NEG)
        return carry

    lax.fori_loop(0, n_chunks, bias_body, 0)

    y_ref[...] = jnp.zeros((D_ATTN, TQ), F32)

    def att_body(c, carry, heads):
        ms, ls = carry
        off = chunk_off(c)
        new_ms, new_ls = [], []
        bias = sc_ref[pl.ds(off, TK), :]
        for n, h in enumerate(heads):
            lane0 = (h // 2) * PAIR
            q2 = qt_ref[0, h * PAIR:(h + 1) * PAIR, :]
            k2 = k_ref[0, pl.ds(off, TK), lane0:lane0 + PAIR]
            s8_ref[n] = jnp.dot(k2, q2, preferred_element_type=F32) + bias
        for n, h in enumerate(heads):
            rows_h = slice(h * HEAD_DIM, (h + 1) * HEAD_DIM)
            s = s8_ref[n]
            m_new = jnp.maximum(ms[n], jnp.max(s, axis=0, keepdims=True))
            p = jnp.exp2(s - m_new)
            alpha = jnp.exp2(ms[n] - m_new)
            new_ms.append(m_new)
            new_ls.append(alpha * ls[n] + jnp.sum(p, axis=0, keepdims=True))
            vc = vt_ref[0, rows_h, pl.ds(off, TK)]
            y_ref[rows_h, :] = alpha * y_ref[rows_h, :] + jnp.dot(
                vc, p.astype(BF16), preferred_element_type=F32)
        return tuple(new_ms), tuple(new_ls)

    ls = []
    for g in range(N_HEADS // HEADS_PER_PASS):
        heads = tuple(range(g * HEADS_PER_PASS, (g + 1) * HEADS_PER_PASS))
        init = (tuple(jnp.full((1, TQ), NEG, F32) for _ in heads),
                tuple(jnp.zeros((1, TQ), F32) for _ in heads))
        _, ls_g = lax.fori_loop(0, n_chunks, functools.partial(att_body, heads=heads), init)
        ls += list(ls_g)
    for h in range(N_HEADS):
        rows_h = slice(h * HEAD_DIM, (h + 1) * HEAD_DIM)
        out = y_ref[rows_h, :] / ls[h]
        ms_h = jnp.mean(out * out, axis=0, keepdims=True)
        y_ref[rows_h, :] = (out * lax.rsqrt(ms_h + EPS)) * gna_ref[rows_h, :]

    ya = (y_ref[...].T * gz_ref[0]).astype(BF16)
    yo = (jnp.dot(yconv_ref[0], wout_ref[0:D_CONV, :], preferred_element_type=F32)
          + jnp.dot(ya, wout_ref[D_CONV:D_CONV + D_ATTN, :], preferred_element_type=F32))
    gate = mod_ref[0][:, 2 * D_MODEL:3 * D_MODEL]
    o = x_ref[0] + gate * yo
    inv = lax.rsqrt(jnp.mean(o * o, axis=-1, keepdims=True) + EPS)
    o_ref[0] = (o * inv) * normf_ref[...]


def _attn_call(qt, qit, wt, k, ki, vt, gz, yconv, x, mod3, gn_attn_col, w_out_bf, norm_f):
    bsz, seq, _ = x.shape
    const = lambda b, i: (0, 0)
    rows = lambda b, i: (b, i, 0)
    cols = lambda b, i: (b, 0, i)
    whole = lambda b, i: (b, 0, 0)
    once = pl.Buffered(1)
    return pl.pallas_call(
        _attn_kernel,
        grid=(bsz, seq // TQ),
        in_specs=[
            pl.BlockSpec((1, N_HEADS * PAIR, TQ), cols),
            pl.BlockSpec((1, N_IDX_HEADS * PAIR, TQ), cols),
            pl.BlockSpec((1, N_IDX_HEADS, TQ), cols),
            pl.BlockSpec((1, seq, D_ATTN), whole),
            pl.BlockSpec((1, seq, PAIR), whole),
            pl.BlockSpec((1, D_ATTN, seq), whole),
            pl.BlockSpec((1, TQ, D_ATTN), rows),
            pl.BlockSpec((1, TQ, D_CONV), rows),
            pl.BlockSpec((1, TQ, D_MODEL), rows),
            pl.BlockSpec((1, 1, 3 * D_MODEL), whole),
            pl.BlockSpec((D_ATTN, 1), const),
            pl.BlockSpec((D_CONV + D_ATTN, D_MODEL), const, pipeline_mode=once),
            pl.BlockSpec((1, D_MODEL), const),
        ],
        out_specs=pl.BlockSpec((1, TQ, D_MODEL), rows),
        out_shape=jax.ShapeDtypeStruct((bsz, seq, D_MODEL), F32),
        scratch_shapes=[pltpu.VMEM((seq, TQ), F32), pltpu.VMEM((seq, TQ), BF16),
                        pltpu.VMEM((D_ATTN, TQ), F32),
                        pltpu.VMEM((HEADS_PER_PASS, TK, TQ), F32), pltpu.VMEM((1, TQ), F32)],
        compiler_params=pltpu.CompilerParams(dimension_semantics=("arbitrary", "arbitrary"),
                                             vmem_limit_bytes=VMEM_LIMIT),
        name="dsa_attention",
    )(qt, qit, wt, k, ki, vt, gz, yconv, x, mod3, gn_attn_col, w_out_bf, norm_f)


def _split_in_proj(w_in):
    conv = w_in[:, 0:2048]
    aq, ak, av, az = (w_in[:, 2048 + n * 512:2560 + n * 512] for n in range(4))
    iq = w_in[:, 4096:4608]
    ik = w_in[:, 4608:4672]
    iw = w_in[:, 4672:4680]
    w_nat = jnp.concatenate([conv, az], axis=1).astype(BF16)
    pad = jnp.zeros((D_MODEL, R_END - R_W - N_IDX_HEADS), w_in.dtype)
    w_cm = jnp.concatenate([aq, ak, av, iq, ik, ik, iw, pad], axis=1).T.astype(BF16)
    return w_nat, w_cm


def kernel(x, c, positions, w_ada, b_ada, norm_in, w_in, conv_w, conv_b, gn_conv, gn_attn, w_out, norm_f):
    bsz, seq, _ = x.shape
    mod3 = _ada_call(c, w_ada, b_ada).reshape(bsz, 1, 3 * D_MODEL)
    w_nat, w_cm = _split_in_proj(w_in)
    group = jnp.arange(D_CONV) // (D_CONV // CONV_GROUPS)
    gmat = jnp.where(group[:, None] == group[None, :], 1.0 / (D_CONV // CONV_GROUPS), 0.0).astype(BF16)
    yconv, gz, qt, k, vt, qit, ki, wt = _proj_call(
        x, positions.reshape(bsz, 1, seq), mod3, norm_in.reshape(1, D_MODEL), w_nat, w_cm,
        conv_w, conv_b.reshape(1, D_CONV), gn_conv.reshape(1, D_CONV), gmat)
    return _attn_call(qt, qit, wt, k, ki, vt, gz, yconv, x, mod3, gn_attn.reshape(D_ATTN, 1),
                      w_out.astype(BF16), norm_f.reshape(1, D_MODEL))
```
